```python
import numpy as np
import jax
import jax.numpy as jnp
from jax import lax

D_MODEL = 4096
BATCH = 4
SEQ = 2048
DEPTH = 4
DEC_BATCH = 8
DEC_SEQ = 4
PAST_LEN = 8192
PAGE_SIZE = 128

A_HEAD_SIZE = 64
A_WIDTH = D_MODEL // 2
A_HEADS = A_WIDTH // A_HEAD_SIZE
A_DECAY_RANK = 96
A_ICLR_RANK = 96
A_GATE_RANK = 256
A_GN_EPS = 64e-5
A_SHIFT_WIDTH = 3 * A_WIDTH + A_DECAY_RANK + A_ICLR_RANK + A_GATE_RANK
A_SPLIT_OFFSETS = (A_WIDTH, 2 * A_WIDTH, 3 * A_WIDTH, 3 * A_WIDTH + A_DECAY_RANK, 3 * A_WIDTH + A_DECAY_RANK + A_ICLR_RANK)
B_WIDTH = D_MODEL // 2
B_BLOCKS = 16
B_BLOCK = B_WIDTH // B_BLOCKS
CONV_WIDTH = 4
LRU_C = 8.0
C_HEADS = 8
C_HEAD_DIM = 128
C_GROUPS = ((128, 1), (512, 4), (2048, 16))
C_N_GROUPS = 3
C_GROUP_WIDTH = C_HEADS * C_HEAD_DIM
C_QKV_WIDTH = C_N_GROUPS * C_GROUP_WIDTH
C_SCALE = C_HEAD_DIM ** -0.5
QK_EPS = 1e-6
D_FF = 11008
NORM_EPS = 1e-6
IN_SPLITS = (A_SHIFT_WIDTH, B_WIDTH, B_WIDTH, C_QKV_WIDTH, C_QKV_WIDTH, C_QKV_WIDTH, D_MODEL, D_MODEL, D_MODEL)
IN_OFFSETS = tuple(int(s) for s in np.cumsum(IN_SPLITS)[:-1])
N_IN = int(sum(IN_SPLITS))
N_STATE = 10

kernel_name = 'hybrid_rwkv7_rglru_dilated_attn_step'


def rmsnorm(x, g):
    xf = x.astype(jnp.float32)
    y = xf * lax.rsqrt(jnp.mean(xf * xf, axis=-1, keepdims=True) + NORM_EPS)
    return (y * g.astype(jnp.float32)).astype(x.dtype)


def swiglu(x, w_gate, w_up, w_down):
    return (jax.nn.silu(x @ w_gate) * (x @ w_up)) @ w_down


def rwkv7_time_mix(pa, shift_prev, wkv0, mu, w0, w_up, a0, a_up, g_up, k_k, k_a, r_k, ln_w, ln_b):
    f32 = jnp.float32
    Bn, T, _ = pa.shape
    prev = jnp.concatenate([shift_prev[:, None, :].astype(pa.dtype), pa[:, :-1]], axis=1)
    xm = pa + (prev - pa) * mu
    r, k, v, zw, za, zg = jnp.split(xm, list(A_SPLIT_OFFSETS), axis=-1)
    logw = -jax.nn.softplus(-(w0 + jnp.tanh(zw) @ w_up).astype(f32)) - 0.5
    decay = jnp.exp(-jnp.exp(logw))
    a = jax.nn.sigmoid((a0 + za @ a_up).astype(f32))
    g = jax.nn.sigmoid(zg) @ g_up
    heads = lambda t: t.astype(f32).reshape(Bn, T, A_HEADS, A_HEAD_SIZE)
    kk = heads(k * k_k)
    kk = kk / jnp.maximum(jnp.sqrt(jnp.sum(kk * kk, axis=-1, keepdims=True)), 1e-12)
    k_h = heads(k.astype(f32) * (1.0 + (a - 1.0) * k_a))
    r_h, w_h, v_h, a_h = heads(r), heads(decay), heads(v), heads(a)
    b_h = kk * a_h

    def step(S, inp):
        r_t, w_t, k_t, v_t, kk_t, b_t = inp
        sa = jnp.einsum('bhvk,bhk->bhv', S, -kk_t)
        S = S * w_t[:, :, None, :] + sa[..., None] * b_t[:, :, None, :] + v_t[..., None] * k_t[:, :, None, :]
        return S, jnp.einsum('bhvk,bhk->bhv', S, r_t)

    xs = tuple(jnp.swapaxes(t, 0, 1) for t in (r_h, w_h, k_h, v_h, kk, b_h))
    S_T, ys = lax.scan(step, wkv0.astype(f32), xs)
    y = jnp.swapaxes(ys, 0, 1)
    mean = jnp.mean(y, axis=-1, keepdims=True)
    var = jnp.mean(jnp.square(y - mean), axis=-1, keepdims=True)
    yn = ((y - mean) * lax.rsqrt(var + A_GN_EPS)).reshape(Bn, T, A_WIDTH) * ln_w + ln_b
    bonus = (jnp.sum(r_h * k_h * r_k, axis=-1, keepdims=True) * v_h).reshape(Bn, T, A_WIDTH)
    out = (yn + bonus) * g
    return out.astype(pa.dtype), pa[:, -1], S_T.astype(wkv0.dtype)


def rglru_mix(xb, gb, conv_prev, h0, conv_w, conv_b, w_a, b_a, w_x, b_x, lam):
    f32 = jnp.float32
    Bn, T, W = xb.shape
    xc = jnp.concatenate([conv_prev.astype(xb.dtype), xb], axis=1)
    c = conv_b + sum(xc[:, j:j + T] * conv_w[j] for j in range(CONV_WIDTH))
    cb = c.reshape(Bn, T, B_BLOCKS, B_BLOCK)
    gate_r = jax.nn.sigmoid((jnp.einsum('btni,nij->btnj', cb, w_a).reshape(Bn, T, W) + b_a).astype(f32))
    gate_i = jax.nn.sigmoid((jnp.einsum('btni,nij->btnj', cb, w_x).reshape(Bn, T, W) + b_x).astype(f32))
    log_a = -LRU_C * gate_r * jax.nn.softplus(-lam.astype(f32))
    a = jnp.exp(log_a)
    u = jnp.sqrt(-jnp.expm1(2.0 * log_a)) * gate_i * c.astype(f32)

    def combine(lhs, rhs):
        return lhs[0] * rhs[0], rhs[0] * lhs[1] + rhs[1]

    a_cum, u_cum = lax.associative_scan(combine, (a, u), axis=1)
    h = a_cum * h0.astype(f32)[:, None, :] + u_cum
    y = h * jax.nn.gelu(gb.astype(f32))
    return y.astype(xb.dtype), xc[:, -(CONV_WIDTH - 1):], h[:, -1].astype(h0.dtype)


def qk_norm(t, g):
    tf = t.astype(jnp.float32)
    return tf * lax.rsqrt(jnp.mean(tf * tf, axis=-1, keepdims=True) + QK_EPS) * g.astype(jnp.float32)


def dilated_prompt(q, k, v, window, dil):
    f32 = jnp.float32
    Bn, T, H, E = q.shape
    band = window // dil
    L = T // dil
    nb = -(-L // band)
    Lp = nb * band

    def to_blocks(t):
        t = t.reshape(Bn, L, dil, H, E).transpose(0, 2, 1, 3, 4)
        t = jnp.pad(t, ((0, 0), (0, 0), (0, Lp - L), (0, 0), (0, 0)))
        return t.reshape(Bn, dil, nb, band, H, E)

    def with_prev(t):
        prev = jnp.pad(t[:, :, :-1], ((0, 0), (0, 0), (1, 0), (0, 0), (0, 0), (0, 0)))
        return jnp.concatenate([prev, t], axis=3)

    qb = to_blocks(q)
    kc = with_prev(to_blocks(k))
    vc = with_prev(to_blocks(v))
    s = jnp.einsum('brnqhe,brnkhe->brnhqk', qb, kc, preferred_element_type=f32) * C_SCALE
    i = jnp.arange(band)[:, None]
    j = jnp.arange(2 * band)[None, :]
    dist = band + i - j
    key_pos = jnp.arange(nb)[:, None, None] * band + j[None] - band
    valid = (dist >= 0)[None] & (dist <= band)[None] & (key_pos >= 0)
    s = jnp.where(valid[None, None, :, None], s, -jnp.inf)
    m = jnp.max(s, axis=-1, keepdims=True)
    p = jnp.exp(s - m)
    den = jnp.sum(p, axis=-1, keepdims=True)
    o = jnp.einsum('brnhqk,brnkhe->brnqhe', p, vc.astype(f32)) / den.transpose(0, 1, 2, 4, 3, 5)
    lse = (m + jnp.log(den))[..., 0].transpose(0, 1, 2, 4, 3)
    o = o.reshape(Bn, dil, Lp, H, E)[:, :, :L].transpose(0, 2, 1, 3, 4).reshape(Bn, T, H, E)
    lse = lse.reshape(Bn, dil, Lp, H)[:, :, :L].transpose(0, 2, 1, 3).reshape(Bn, T, H)
    return o, lse


def dilated_sample(q, k_new, v_new, k_buf, v_buf, window, dil):
    f32 = jnp.float32
    n = q.shape[1]
    Lb = k_buf.shape[1]
    band = window // dil
    k_all = jnp.concatenate([k_buf.astype(k_new.dtype), k_new], axis=1)
    v_all = jnp.concatenate([v_buf.astype(v_new.dtype), v_new], axis=1)
    idx = Lb + jnp.arange(n)[:, None] - dil * jnp.arange(band + 1)[None, :]
    valid = idx >= 0
    idx = jnp.maximum(idx, 0)
    kg = k_all[:, idx]
    vg = v_all[:, idx]
    s = jnp.einsum('bqhe,bqkhe->bhqk', q, kg, preferred_element_type=f32) * C_SCALE
    s = jnp.where(valid[None, None], s, -jnp.inf)
    m = jnp.max(s, axis=-1, keepdims=True)
    p = jnp.exp(s - m)
    den = jnp.sum(p, axis=-1, keepdims=True)
    o = jnp.einsum('bhqk,bqkhe->bqhe', p, vg.astype(f32)) / jnp.swapaxes(den, 1, 2)
    lse = jnp.swapaxes((m + jnp.log(den))[..., 0], 1, 2)
    return o, lse, k_all[:, -Lb:], v_all[:, -Lb:]


def dilated_attention(q, k, v, q_gain, k_gain, kv_bufs, is_prompt):
    Bn, T, _ = q.shape
    shp = (Bn, T, C_N_GROUPS, C_HEADS, C_HEAD_DIM)
    qn = qk_norm(q.reshape(shp), q_gain[:, None, :]).astype(q.dtype)
    kn = qk_norm(k.reshape(shp), k_gain[:, None, :]).astype(k.dtype)
    vr = v.reshape(shp)
    outs, lses, new_bufs = [], [], []
    for gi, (window, dil) in enumerate(C_GROUPS):
        qg, kg, vg = qn[:, :, gi], kn[:, :, gi], vr[:, :, gi]
        if is_prompt:
            o, lse = dilated_prompt(qg, kg, vg, window, dil)
            keep = min(window, T)
            new_bufs += [kg[:, T - keep:], vg[:, T - keep:]]
        else:
            o, lse, k_keep, v_keep = dilated_sample(qg, kg, vg, kv_bufs[2 * gi], kv_bufs[2 * gi + 1], window, dil)
            new_bufs += [k_keep, v_keep]
        outs.append(o)
        lses.append(lse)
    wts = jax.nn.softmax(jnp.stack(lses, axis=-1), axis=-1)
    o = jnp.einsum('bthg,bthge->bthe', wts, jnp.stack(outs, axis=-2)).reshape(Bn, T, C_GROUP_WIDTH)
    return o.astype(q.dtype), new_bufs


def trunk_layer(x, l, prm, st, is_prompt):
    shift_prev, wkv0, conv_prev, h0, kv_bufs = st
    h = x + 0.5 * swiglu(rmsnorm(x, prm['norm_ffn1'][l]), prm['ffn1_w_gate'][l], prm['ffn1_w_up'][l], prm['ffn1_w_down'][l])
    u = rmsnorm(h, prm['norm_mix'][l])
    proj = u @ prm['w_in'][l]
    pa, lru_x, lru_gate, q, k, v, gate_a, gate_b, gate_c = jnp.split(proj, list(IN_OFFSETS), axis=-1)
    o_a, shift_new, wkv_new = rwkv7_time_mix(
        pa, shift_prev, wkv0, prm['rwkv_mu'][l], prm['rwkv_w0'][l], prm['rwkv_w_up'][l], prm['rwkv_a0'][l],
        prm['rwkv_a_up'][l], prm['rwkv_g_up'][l], prm['rwkv_k_k'][l], prm['rwkv_k_a'][l], prm['rwkv_r_k'][l],
        prm['rwkv_ln_w'][l], prm['rwkv_ln_b'][l])
    o_b, conv_new, h_new = rglru_mix(
        lru_x, lru_gate, conv_prev, h0, prm['lru_conv_w'][l], prm['lru_conv_b'][l], prm['lru_w_a'][l],
        prm['lru_b_a'][l], prm['lru_w_x'][l], prm['lru_b_x'][l], prm['lru_lambda'][l])
    o_c, kv_new = dilated_attention(q, k, v, prm['attn_q_norm'][l], prm['attn_k_norm'][l], kv_bufs, is_prompt)
    merged = (jax.nn.sigmoid(gate_a) * (o_a @ prm['w_br_a'][l])
              + jax.nn.sigmoid(gate_b) * (o_b @ prm['w_br_b'][l])
              + jax.nn.sigmoid(gate_c) * (o_c @ prm['w_br_c'][l]))
    h = h + merged @ prm['w_out'][l]
    y = h + 0.5 * swiglu(rmsnorm(h, prm['norm_ffn2'][l]), prm['ffn2_w_gate'][l], prm['ffn2_w_up'][l], prm['ffn2_w_down'][l])
    return y, (shift_new, wkv_new, conv_new, h_new, *kv_new)


def setup_inputs(seed: int = 0) -> dict:
    key = jax.random.key(seed)
    ks = iter(jax.random.split(key, 64))
    f32 = jnp.float32

    def nrm(shape, scale):
        return jax.random.normal(next(ks), shape, f32) * scale

    def unif(shape, lo, hi):
        return jax.random.uniform(next(ks), shape, f32, lo, hi)

    L = DEPTH
    buf_lens = [min(w, PAST_LEN) for w, _ in C_GROUPS]
    inp = {}
    inp['x_prompt'] = nrm((BATCH, SEQ, D_MODEL), 1.0)
    inp['x_sample'] = nrm((DEC_BATCH, DEC_SEQ, D_MODEL), 1.0)
    inp['state_rwkv_shift'] = nrm((L, DEC_BATCH, A_SHIFT_WIDTH), 1.0)
    inp['state_rwkv_wkv'] = nrm((L, DEC_BATCH, A_HEADS, A_HEAD_SIZE, A_HEAD_SIZE), 0.1)
    inp['state_lru_conv'] = nrm((L, DEC_BATCH, CONV_WIDTH - 1, B_WIDTH), 1.0)
    inp['state_lru_h'] = nrm((L, DEC_BATCH, B_WIDTH), 0.5)
    for gi in range(C_N_GROUPS):
        inp['cache_dil%d_k' % (gi + 1)] = nrm((L, DEC_BATCH, buf_lens[gi], C_HEADS, C_HEAD_DIM), 1.0)
        inp['cache_dil%d_v' % (gi + 1)] = nrm((L, DEC_BATCH, buf_lens[gi], C_HEADS, C_HEAD_DIM), 1.0)
    inp['norm_ffn1'] = 1.0 + nrm((L, D_MODEL), 0.02)
    inp['ffn1_w_gate'] = nrm((L, D_MODEL, D_FF), D_MODEL ** -0.5)
    inp['ffn1_w_up'] = nrm((L, D_MODEL, D_FF), D_MODEL ** -0.5)
    inp['ffn1_w_down'] = nrm((L, D_FF, D_MODEL), D_FF ** -0.5)
    inp['norm_mix'] = 1.0 + nrm((L, D_MODEL), 0.02)
    inp['w_in'] = nrm((L, D_MODEL, N_IN), D_MODEL ** -0.5)
    inp['rwkv_mu'] = unif((L, A_SHIFT_WIDTH), 0.0, 1.0)
    inp['rwkv_w0'] = unif((L, A_WIDTH), -6.0, 1.0)
    inp['rwkv_w_up'] = nrm((L, A_DECAY_RANK, A_WIDTH), 0.1 * A_DECAY_RANK ** -0.5)
    inp['rwkv_a0'] = nrm((L, A_WIDTH), 0.1)
    inp['rwkv_a_up'] = nrm((L, A_ICLR_RANK, A_WIDTH), 0.5 * A_ICLR_RANK ** -0.5)
    inp['rwkv_g_up'] = nrm((L, A_GATE_RANK, A_WIDTH), A_GATE_RANK ** -0.5)
    inp['rwkv_k_k'] = 1.0 + nrm((L, A_WIDTH), 0.1)
    inp['rwkv_k_a'] = 1.0 + nrm((L, A_WIDTH), 0.1)
    inp['rwkv_r_k'] = nrm((L, A_HEADS, A_HEAD_SIZE), 0.1)
    inp['rwkv_ln_w'] = 1.0 + nrm((L, A_WIDTH), 0.02)
    inp['rwkv_ln_b'] = nrm((L, A_WIDTH), 0.01)
    inp['lru_conv_w'] = nrm((L, CONV_WIDTH, B_WIDTH), CONV_WIDTH ** -0.5)
    inp['lru_conv_b'] = nrm((L, B_WIDTH), 0.01)
    inp['lru_w_a'] = nrm((L, B_BLOCKS, B_BLOCK, B_BLOCK), B_BLOCK ** -0.5)
    inp['lru_b_a'] = nrm((L, B_WIDTH), 0.01)
    inp['lru_w_x'] = nrm((L, B_BLOCKS, B_BLOCK, B_BLOCK), B_BLOCK ** -0.5)
    inp['lru_b_x'] = nrm((L, B_WIDTH), 0.01)
    inp['lru_lambda'] = unif((L, B_WIDTH), 4.3, 9.0)
    inp['attn_q_norm'] = 1.0 + nrm((L, C_N_GROUPS, C_HEAD_DIM), 0.02)
    inp['attn_k_norm'] = 1.0 + nrm((L, C_N_GROUPS, C_HEAD_DIM), 0.02)
    inp['w_br_a'] = nrm((L, A_WIDTH, D_MODEL), A_WIDTH ** -0.5)
    inp['w_br_b'] = nrm((L, B_WIDTH, D_MODEL), B_WIDTH ** -0.5)
    inp['w_br_c'] = nrm((L, C_GROUP_WIDTH, D_MODEL), C_GROUP_WIDTH ** -0.5)
    inp['w_out'] = nrm((L, D_MODEL, D_MODEL), D_MODEL ** -0.5)
    inp['norm_ffn2'] = 1.0 + nrm((L, D_MODEL), 0.02)
    inp['ffn2_w_gate'] = nrm((L, D_MODEL, D_FF), D_MODEL ** -0.5)
    inp['ffn2_w_up'] = nrm((L, D_MODEL, D_FF), D_MODEL ** -0.5)
    inp['ffn2_w_down'] = nrm((L, D_FF, D_MODEL), D_FF ** -0.5)
    return inp


def _stack_layers(outs, i):
    return jnp.stack([o[i] for o in outs])


def reference(x_prompt, x_sample, state_rwkv_shift, state_rwkv_wkv, state_lru_conv, state_lru_h,
              cache_dil1_k, cache_dil1_v, cache_dil2_k, cache_dil2_v, cache_dil3_k, cache_dil3_v,
              norm_ffn1, ffn1_w_gate, ffn1_w_up, ffn1_w_down, norm_mix, w_in,
              rwkv_mu, rwkv_w0, rwkv_w_up, rwkv_a0, rwkv_a_up, rwkv_g_up, rwkv_k_k, rwkv_k_a, rwkv_r_k,
              rwkv_ln_w, rwkv_ln_b, lru_conv_w, lru_conv_b, lru_w_a, lru_b_a, lru_w_x, lru_b_x, lru_lambda,
              attn_q_norm, attn_k_norm, w_br_a, w_br_b, w_br_c, w_out,
              norm_ffn2, ffn2_w_gate, ffn2_w_up, ffn2_w_down):
    prm = dict(norm_ffn1=norm_ffn1, ffn1_w_gate=ffn1_w_gate, ffn1_w_up=ffn1_w_up, ffn1_w_down=ffn1_w_down,
               norm_mix=norm_mix, w_in=w_in, rwkv_mu=rwkv_mu, rwkv_w0=rwkv_w0, rwkv_w_up=rwkv_w_up,
               rwkv_a0=rwkv_a0, rwkv_a_up=rwkv_a_up, rwkv_g_up=rwkv_g_up, rwkv_k_k=rwkv_k_k, rwkv_k_a=rwkv_k_a,
               rwkv_r_k=rwkv_r_k, rwkv_ln_w=rwkv_ln_w, rwkv_ln_b=rwkv_ln_b, lru_conv_w=lru_conv_w,
               lru_conv_b=lru_conv_b, lru_w_a=lru_w_a, lru_b_a=lru_b_a, lru_w_x=lru_w_x, lru_b_x=lru_b_x,
               lru_lambda=lru_lambda, attn_q_norm=attn_q_norm, attn_k_norm=attn_k_norm, w_br_a=w_br_a,
               w_br_b=w_br_b, w_br_c=w_br_c, w_out=w_out, norm_ffn2=norm_ffn2, ffn2_w_gate=ffn2_w_gate,
               ffn2_w_up=ffn2_w_up, ffn2_w_down=ffn2_w_down)
    Bp = x_prompt.shape[0]
    dt = x_prompt.dtype
    cache_bufs = (cache_dil1_k, cache_dil1_v, cache_dil2_k, cache_dil2_v, cache_dil3_k, cache_dil3_v)
    y_prompt, y_sample = x_prompt, x_sample
    outs_p, outs_s = [], []
    for l in range(DEPTH):
        st_p = (jnp.zeros((Bp, A_SHIFT_WIDTH), dt),
                jnp.zeros((Bp, A_HEADS, A_HEAD_SIZE, A_HEAD_SIZE), state_rwkv_wkv.dtype),
                jnp.zeros((Bp, CONV_WIDTH - 1, B_WIDTH), dt),
                jnp.zeros((Bp, B_WIDTH), state_lru_h.dtype),
                None)
        y_prompt, new_p = trunk_layer(y_prompt, l, prm, st_p, True)
        st_s = (state_rwkv_shift[l], state_rwkv_wkv[l], state_lru_conv[l], state_lru_h[l],
                tuple(c[l] for c in cache_bufs))
        y_sample, new_s = trunk_layer(y_sample, l, prm, st_s, False)
        outs_p.append(new_p)
        outs_s.append(new_s)
    p_shift, s_shift = _stack_layers(outs_p, 0), _stack_layers(outs_s, 0)
    p_wkv, s_wkv = _stack_layers(outs_p, 1), _stack_layers(outs_s, 1)
    p_conv, s_conv = _stack_layers(outs_p, 2), _stack_layers(outs_s, 2)
    p_h, s_h = _stack_layers(outs_p, 3), _stack_layers(outs_s, 3)
    p_k1, s_k1 = _stack_layers(outs_p, 4), _stack_layers(outs_s, 4)
    p_v1, s_v1 = _stack_layers(outs_p, 5), _stack_layers(outs_s, 5)
    p_k2, s_k2 = _stack_layers(outs_p, 6), _stack_layers(outs_s, 6)
    p_v2, s_v2 = _stack_layers(outs_p, 7), _stack_layers(outs_s, 7)
    p_k3, s_k3 = _stack_layers(outs_p, 8), _stack_layers(outs_s, 8)
    p_v3, s_v3 = _stack_layers(outs_p, 9), _stack_layers(outs_s, 9)
    return (y_prompt, y_sample, p_shift, s_shift, p_wkv, s_wkv, p_conv, s_conv, p_h, s_h,
            p_k1, s_k1, p_v1, s_v1, p_k2, s_k2, p_v2, s_v2, p_k3, s_k3, p_v3, s_v3)
```

```python
import functools

import numpy as np
import jax
import jax.numpy as jnp
from jax import lax
from jax.experimental import pallas as pl
from jax.experimental.pallas import tpu as pltpu

F32 = jnp.float32
BF16 = jnp.bfloat16

D_MODEL = 4096
BATCH = 4
SEQ = 2048
DEPTH = 4
DEC_BATCH = 8
DEC_SEQ = 4
PAST_LEN = 8192

A_HEAD_SIZE = 64
A_WIDTH = D_MODEL // 2
A_HEADS = A_WIDTH // A_HEAD_SIZE
A_DECAY_RANK = 96
A_ICLR_RANK = 96
A_GATE_RANK = 256
A_GN_EPS = 64e-5
A_SHIFT_WIDTH = 3 * A_WIDTH + A_DECAY_RANK + A_ICLR_RANK + A_GATE_RANK
A_SPLIT_OFFSETS = (A_WIDTH, 2 * A_WIDTH, 3 * A_WIDTH, 3 * A_WIDTH + A_DECAY_RANK,
                   3 * A_WIDTH + A_DECAY_RANK + A_ICLR_RANK)
B_WIDTH = D_MODEL // 2
B_BLOCKS = 16
B_BLOCK = B_WIDTH // B_BLOCKS
CONV_WIDTH = 4
LRU_C = 8.0
C_HEADS = 8
C_HEAD_DIM = 128
C_GROUPS = ((128, 1), (512, 4), (2048, 16))
C_N_GROUPS = 3
C_GROUP_WIDTH = C_HEADS * C_HEAD_DIM
C_QKV_WIDTH = C_N_GROUPS * C_GROUP_WIDTH
C_SCALE = C_HEAD_DIM ** -0.5
QK_EPS = 1e-6
D_FF = 11008
NORM_EPS = 1e-6
IN_SPLITS = (A_SHIFT_WIDTH, B_WIDTH, B_WIDTH, C_QKV_WIDTH, C_QKV_WIDTH, C_QKV_WIDTH, D_MODEL, D_MODEL, D_MODEL)
IN_OFFSETS = tuple(int(s) for s in np.cumsum(IN_SPLITS)[:-1])

LANE = 128
VMEM_LIMIT_BYTES = 56 * 1024 * 1024

N_PROMPT = BATCH * SEQ
N_SAMPLE = DEC_BATCH * DEC_SEQ
ROW_TILE = 1040
M_PAD = 8 * ROW_TILE
D_FF_PAD = 11 * 1024
A_SHIFT_PAD = 13 * 512

assert M_PAD >= N_PROMPT + N_SAMPLE


def _cparams(sem):
    return pltpu.CompilerParams(dimension_semantics=sem, vmem_limit_bytes=VMEM_LIMIT_BYTES)


def _rmsnorm_kernel(x_ref, g_ref, o_ref):
    x = x_ref[...]
    ms = jnp.mean(x * x, axis=-1, keepdims=True)
    o_ref[...] = (x * lax.rsqrt(ms + NORM_EPS) * g_ref[...]).astype(o_ref.dtype)


def rmsnorm_rows(x, g, layer):
    m, d = x.shape
    tm = 208
    return pl.pallas_call(
        _rmsnorm_kernel,
        grid=(m // tm,),
        in_specs=[pl.BlockSpec((tm, d), lambda i: (i, 0)),
                  pl.BlockSpec((None, 1, d), lambda i: (layer, 0, 0))],
        out_specs=pl.BlockSpec((tm, d), lambda i: (i, 0)),
        out_shape=jax.ShapeDtypeStruct((m, d), BF16),
        compiler_params=_cparams(("parallel",)),
        name="rmsnorm",
    )(x, g)


def _mm_kernel(x_ref, w_ref, o_ref):
    o_ref[...] = jnp.dot(x_ref[...], w_ref[...], preferred_element_type=F32).astype(o_ref.dtype)


def matmul_fullk(x, w, layer, tn, out_dtype, name):
    m, k = x.shape
    n = w.shape[2]
    return pl.pallas_call(
        _mm_kernel,
        grid=(m // ROW_TILE, n // tn),
        in_specs=[pl.BlockSpec((ROW_TILE, k), lambda i, j: (i, 0)),
                  pl.BlockSpec((None, k, tn), lambda i, j: (layer, 0, j))],
        out_specs=pl.BlockSpec((ROW_TILE, tn), lambda i, j: (i, j)),
        out_shape=jax.ShapeDtypeStruct((m, n), out_dtype),
        compiler_params=_cparams(("parallel", "parallel")),
        name=name,
    )(x, w)


def _qknorm_mm_kernel(x_ref, w_ref, g_ref, o_ref):
    acc = jnp.dot(x_ref[...], w_ref[...], preferred_element_type=F32)
    gain = g_ref[...]
    for h in range(acc.shape[1] // C_HEAD_DIM):
        ls = slice(h * C_HEAD_DIM, (h + 1) * C_HEAD_DIM)
        t = acc[:, ls]
        o_ref[:, ls] = t * lax.rsqrt(jnp.mean(t * t, axis=-1, keepdims=True) + QK_EPS) * gain


def matmul_qknorm(x, w, gain, layer, name):
    m, k = x.shape
    n = w.shape[2]
    tn = C_GROUP_WIDTH
    return pl.pallas_call(
        _qknorm_mm_kernel,
        grid=(m // ROW_TILE, n // tn),
        in_specs=[pl.BlockSpec((ROW_TILE, k), lambda i, j: (i, 0)),
                  pl.BlockSpec((None, k, tn), lambda i, j: (layer, 0, j)),
                  pl.BlockSpec((None, None, 1, C_HEAD_DIM), lambda i, j: (layer, j, 0, 0))],
        out_specs=pl.BlockSpec((ROW_TILE, tn), lambda i, j: (i, j)),
        out_shape=jax.ShapeDtypeStruct((m, n), F32),
        compiler_params=_cparams(("parallel", "parallel")),
        name=name,
    )(x, w, gain)


def _gateup_kernel(x_ref, wg_ref, wu_ref, o_ref):
    x = x_ref[...]
    g = jnp.dot(x, wg_ref[...], preferred_element_type=F32)
    u = jnp.dot(x, wu_ref[...], preferred_element_type=F32)
    o_ref[...] = (g * jax.nn.sigmoid(g) * u).astype(o_ref.dtype)


def ffn_gateup(x, wg, wu, layer):
    m, k = x.shape
    n = wg.shape[2]
    tn = 512
    return pl.pallas_call(
        _gateup_kernel,
        grid=(m // ROW_TILE, n // tn),
        in_specs=[pl.BlockSpec((ROW_TILE, k), lambda i, j: (i, 0)),
                  pl.BlockSpec((None, k, tn), lambda i, j: (layer, 0, j)),
                  pl.BlockSpec((None, k, tn), lambda i, j: (layer, 0, j))],
        out_specs=pl.BlockSpec((ROW_TILE, tn), lambda i, j: (i, j)),
        out_shape=jax.ShapeDtypeStruct((m, n), BF16),
        compiler_params=_cparams(("parallel", "parallel")),
        name="ffn_gateup",
    )(x, wg, wu)


def _mm_resid_kernel(x_ref, w_ref, r_ref, o_ref, *, scale):
    p = scale * jnp.dot(x_ref[...], w_ref[...], preferred_element_type=F32)

    @pl.when(pl.program_id(2) == 0)
    def _():
        o_ref[...] = r_ref[...] + p

    @pl.when(pl.program_id(2) > 0)
    def _():
        o_ref[...] += p


def matmul_resid(x, w, resid, layer, tn, tk, scale, name):
    m, k = x.shape
    n = w.shape[2]
    return pl.pallas_call(
        functools.partial(_mm_resid_kernel, scale=scale),
        grid=(m // ROW_TILE, n // tn, k // tk),
        in_specs=[pl.BlockSpec((ROW_TILE, tk), lambda i, j, kk: (i, kk)),
                  pl.BlockSpec((None, tk, tn), lambda i, j, kk: (layer, kk, j)),
                  pl.BlockSpec((ROW_TILE, tn), lambda i, j, kk: (i, j))],
        out_specs=pl.BlockSpec((ROW_TILE, tn), lambda i, j, kk: (i, j)),
        out_shape=jax.ShapeDtypeStruct((m, n), F32),
        compiler_params=_cparams(("parallel", "parallel", "arbitrary")),
        name=name,
    )(x, w, resid)


def _merge_kernel(oa_ref, ob_ref, oc_ref, wa_ref, wb_ref, wc_ref, ga_ref, gb_ref, gc_ref, o_ref):
    a = jnp.dot(oa_ref[...], wa_ref[...], preferred_element_type=F32)
    b = jnp.dot(ob_ref[...], wb_ref[...], preferred_element_type=F32)
    c = jnp.dot(oc_ref[...], wc_ref[...], preferred_element_type=F32)
    o_ref[...] = (jax.nn.sigmoid(ga_ref[...]) * a + jax.nn.sigmoid(gb_ref[...]) * b
                  + jax.nn.sigmoid(gc_ref[...]) * c).astype(o_ref.dtype)


def branch_merge(oa, ob, oc, wa, wb, wc, gates, layer):
    m = oa.shape[0]
    n = wa.shape[2]
    tn = 512
    nb = n // tn
    row = lambda i, j: (i, 0)
    wcol = lambda i, j: (layer, 0, j)
    return pl.pallas_call(
        _merge_kernel,
        grid=(m // ROW_TILE, nb),
        in_specs=[pl.BlockSpec((ROW_TILE, oa.shape[1]), row),
                  pl.BlockSpec((ROW_TILE, ob.shape[1]), row),
                  pl.BlockSpec((ROW_TILE, oc.shape[1]), row),
                  pl.BlockSpec((None, wa.shape[1], tn), wcol),
                  pl.BlockSpec((None, wb.shape[1], tn), wcol),
                  pl.BlockSpec((None, wc.shape[1], tn), wcol),
                  pl.BlockSpec((ROW_TILE, tn), lambda i, j: (i, j)),
                  pl.BlockSpec((ROW_TILE, tn), lambda i, j: (i, nb + j)),
                  pl.BlockSpec((ROW_TILE, tn), lambda i, j: (i, 2 * nb + j))],
        out_specs=pl.BlockSpec((ROW_TILE, tn), lambda i, j: (i, j)),
        out_shape=jax.ShapeDtypeStruct((m, n), BF16),
        compiler_params=_cparams(("parallel", "parallel")),
        name="branch_merge",
    )(oa, ob, oc, wa, wb, wc, gates, gates, gates)


RW_C = 64
RW_L = 256
RW_HS = A_HEAD_SIZE
PA_W = A_SHIFT_PAD


def _dot(a, b):
    return jnp.dot(a.astype(BF16), b.astype(BF16), preferred_element_type=F32)


def _dot_nt(a, b):
    return lax.dot_general(a.astype(BF16), b.astype(BF16), (((1,), (1,)), ((), ())), preferred_element_type=F32)


def _dot_tn(a, b):
    return lax.dot_general(a.astype(BF16), b.astype(BF16), (((0,), (0,)), ((), ())), preferred_element_type=F32)


def _split_dot(e, x, parts):
    acc = None
    rem = x
    for _ in range(parts):
        p = rem.astype(BF16)
        rem = rem - p.astype(F32)
        d = jnp.dot(e, p, preferred_element_type=F32)
        acc = d if acc is None else acc + d
    return acc


def _split_dot_r(x, e, parts):
    acc = None
    rem = x
    for _ in range(parts):
        p = rem.astype(BF16)
        rem = rem - p.astype(F32)
        d = jnp.dot(p, e, preferred_element_type=F32)
        acc = d if acc is None else acc + d
    return acc


def _softplus(x):
    return jnp.maximum(x, 0.0) + jnp.log1p(jnp.exp(-jnp.abs(x)))


def rwkv_prep(r, k, v, zw, za, zg, w0, w_up, a0, a_up, g_up, k_k, k_a, r_k, seg):
    logw = -_softplus(-(w0 + _dot(jnp.tanh(zw), w_up))) - 0.5
    ld = -jnp.exp(logw)
    a = jax.nn.sigmoid(a0 + _dot(za, a_up))
    g = _dot(jax.nn.sigmoid(zg), g_up)
    kk = k * k_k
    kk = kk / jnp.maximum(jnp.sqrt(_split_dot_r(kk * kk, seg, 2)), 1e-12)
    k_h = k * (1.0 + (a - 1.0) * k_a)
    bonus = _split_dot_r(r * k_h * r_k, seg, 2) * v
    return ld, -kk, kk * a, k_h, g, bonus


def _rwkv_kernel(r_ref, k_ref, v_ref, wa_ref, zg_ref, sh_r, sh_k, sh_v, sh_wa, sh_g, s0_ref,
                 mu_r, mu_k, mu_v, mu_wa, mu_g,
                 w0_ref, wup_ref, a0_ref, aup_ref, gup_ref, kk_ref, ka_ref, rk_ref, lnw_ref, lnb_ref,
                 o_ref, s_ref,
                 at_s, rh_s, nrb_s, bt_s, kt_s, vb_s, uv_s, yv_s, bon_s, g_s, gc_s, *, n_chunks, n_valid):
    C, L = RW_C, RW_L
    row = lax.broadcasted_iota(jnp.int32, (C, 4 * C), 0)
    col = lax.broadcasted_iota(jnp.int32, (C, 4 * C), 1) % C
    strict = col < row
    incl = col <= row
    eye_cat = (col == row).astype(F32)
    bd_mask = (lax.broadcasted_iota(jnp.int32, (L, L), 0) // RW_HS) == (lax.broadcasted_iota(jnp.int32, (L, L), 1) // RW_HS)
    seg = bd_mask.astype(BF16)
    ltri = (lax.broadcasted_iota(jnp.int32, (C, C), 1) <= lax.broadcasted_iota(jnp.int32, (C, C), 0)).astype(BF16)
    first_row = lax.broadcasted_iota(jnp.int32, (C, L), 0) == 0

    def stack4(z):
        zb = z.astype(BF16)
        return jnp.where(bd_mask, jnp.concatenate([zb, zb, zb, zb], axis=0), jnp.zeros((), BF16))

    def shifted(x_ref, mu_ref, r0, last):
        x = x_ref[pl.ds(r0, C), :]
        prev = jnp.where(first_row, last, pltpu.roll(x, 1, axis=0))
        return x + (prev - x) * mu_ref[...], x[C - 1:C, :]

    def phase1(c, carry):
        r0 = pl.multiple_of(c * C, C)
        xr, l_r = shifted(r_ref, mu_r, r0, carry[0])
        xk, l_k = shifted(k_ref, mu_k, r0, carry[1])
        xv, l_v = shifted(v_ref, mu_v, r0, carry[2])
        xwa, l_wa = shifted(wa_ref, mu_wa, r0, carry[3])
        xg, l_g = shifted(zg_ref, mu_g, r0, carry[4])
        ld, a_vec, b_vec, k_h, g, bonus = rwkv_prep(
            xr, xk, xv, xwa[:, :128], xwa[:, 128:], xg, w0_ref[...], wup_ref[...], a0_ref[...], aup_ref[...],
            gup_ref[...], kk_ref[...], ka_ref[...], rk_ref[...], seg)
        if n_valid < n_chunks * C:
            live = lax.broadcasted_iota(jnp.int32, (C, L), 0) < n_valid
            ld, a_vec, b_vec, k_h = (jnp.where(live, t_, 0.0) for t_ in (ld, a_vec, b_vec, k_h))
        cum = _split_dot(ltri, ld, 3)
        cum_c = cum[C - 1:C, :]
        ah = a_vec * jnp.exp(cum - ld)
        rh = xr * jnp.exp(cum)
        inv = jnp.exp(-cum)
        rel = jnp.exp(cum_c - cum)
        ar = jnp.concatenate([ah, rh], axis=0)
        pb = _dot_nt(ar, stack4(b_vec * inv))
        pk = _dot_nt(ar, stack4(k_h * inv))
        nab = jnp.where(strict, pb[:C], 0.0)
        nrb = jnp.where(incl, pb[C:], 0.0)
        nak = jnp.where(strict, pk[:C], 0.0)
        nrk = jnp.where(incl, pk[C:], 0.0)
        t = eye_cat + nab
        n = _dot(nab, stack4(nab))
        for _ in range(5):
            p = _dot(jnp.concatenate([n, t], axis=0), stack4(n))
            t = t + p[C:]
            n = p[:C]
        at = _dot(t, stack4(ah))
        nakv = _dot(nak, stack4(xv))
        uv = _dot(t, stack4(nakv))
        yv = _dot(nrk, stack4(xv))
        at_s[c] = at.astype(BF16)
        rh_s[c] = rh.astype(BF16)
        nrb_s[c] = nrb.astype(BF16)
        bt_s[c] = (b_vec * rel).astype(BF16)
        kt_s[c] = (k_h * rel).astype(BF16)
        vb_s[c] = xv.astype(BF16)
        uv_s[c] = uv
        yv_s[c] = yv
        bon_s[c] = bonus
        g_s[c] = g
        gc_s[c] = jnp.broadcast_to(jnp.exp(cum_c), (8, L))
        return (l_r, l_k, l_v, l_wa, l_g)

    lax.fori_loop(0, n_chunks, phase1, (sh_r[...], sh_k[...], sh_v[...], sh_wa[...], sh_g[...]))

    def phase2(c, sbd):
        sb = sbd.astype(BF16)
        u = _dot_nt(at_s[c], sb) + uv_s[c]
        y = _dot_nt(rh_s[c], sb) + _dot(nrb_s[c], stack4(u)) + yv_s[c]
        upd = _dot_tn(jnp.concatenate([u.astype(BF16), vb_s[c]], axis=0),
                      jnp.concatenate([bt_s[c], kt_s[c]], axis=0))
        sbd = gc_s[c][0:1, :] * sbd + jnp.where(bd_mask, upd, 0.0)
        mean = _split_dot_r(y, seg, 2) * (1.0 / RW_HS)
        d = y - mean
        var = _split_dot_r(d * d, seg, 2) * (1.0 / RW_HS)
        yn = d * lax.rsqrt(var + A_GN_EPS) * lnw_ref[...] + lnb_ref[...]
        r0 = pl.multiple_of(c * C, C)
        o_ref[pl.ds(r0, C), :] = ((yn + bon_s[c]) * g_s[c]).astype(o_ref.dtype)
        return sbd

    s_ref[...] = lax.fori_loop(0, n_chunks, phase2, s0_ref[...])


def rwkv_mix(pa, shift0, s0, layer, mu, w0, w_up, a0, a_up, g_up, k_k, k_a, r_k, ln_w, ln_b, *, batch, seq, n_valid,
             out_rows):
    n_hg = A_WIDTH // RW_L
    nc = seq // RW_C
    col = lambda off: (lambda b, h: (b, off + h))
    fixed = lambda off: (lambda b, h: (b, off))
    pcol = lambda off: (lambda b, h: (layer, 0, off + h))
    pfix = lambda off: (lambda b, h: (layer, 0, off))
    scol = lambda off: (lambda b, h: (b, 0, off + h))
    sfix = lambda off: (lambda b, h: (b, 0, off))
    blk = lambda im: pl.BlockSpec((seq, RW_L), im)
    pblk = lambda rows, im: pl.BlockSpec((None, rows, RW_L), im)
    cs = lambda dt, *shape: pltpu.VMEM((nc,) + shape, dt)
    return pl.pallas_call(
        functools.partial(_rwkv_kernel, n_chunks=nc, n_valid=n_valid),
        grid=(batch, n_hg),
        in_specs=[blk(col(0)), blk(col(n_hg)), blk(col(2 * n_hg)), blk(fixed(3 * n_hg)), blk(fixed(3 * n_hg + 1)),
                  pblk(1, scol(0)), pblk(1, scol(n_hg)), pblk(1, scol(2 * n_hg)), pblk(1, sfix(3 * n_hg)),
                  pblk(1, sfix(3 * n_hg + 1)),
                  pl.BlockSpec((None, None, RW_L, RW_L), lambda b, h: (b, h, 0, 0)),
                  pblk(1, pcol(0)), pblk(1, pcol(n_hg)), pblk(1, pcol(2 * n_hg)), pblk(1, pfix(3 * n_hg)),
                  pblk(1, pfix(3 * n_hg + 1)),
                  pblk(1, pcol(0)), pblk(128, pcol(0)), pblk(1, pcol(0)), pblk(128, pcol(0)), pblk(256, pcol(0)),
                  pblk(1, pcol(0)), pblk(1, pcol(0)), pblk(1, pcol(0)), pblk(1, pcol(0)), pblk(1, pcol(0))],
        out_specs=[pl.BlockSpec((seq, RW_L), lambda b, h: (b, h)),
                   pl.BlockSpec((None, None, RW_L, RW_L), lambda b, h: (b, h, 0, 0))],
        out_shape=[jax.ShapeDtypeStruct((out_rows, A_WIDTH), BF16),
                   jax.ShapeDtypeStruct((batch, n_hg, RW_L, RW_L), F32)],
        scratch_shapes=[cs(BF16, RW_C, RW_L), cs(BF16, RW_C, RW_L), cs(BF16, RW_C, 4 * RW_C), cs(BF16, RW_C, RW_L),
                        cs(BF16, RW_C, RW_L), cs(BF16, RW_C, RW_L), cs(F32, RW_C, RW_L), cs(F32, RW_C, RW_L),
                        cs(F32, RW_C, RW_L), cs(F32, RW_C, RW_L), cs(F32, 8, RW_L)],
        compiler_params=_cparams(("parallel", "parallel")),
        name="rwkv_mix",
    )(pa, pa, pa, pa, pa, shift0, shift0, shift0, shift0, shift0, s0, mu, mu, mu, mu, mu,
      w0, w_up, a0, a_up, g_up, k_k, k_a, r_k, ln_w, ln_b)


LRU_ROWS = 64
LRU_L = 256


def _gelu_tanh(x):
    return 0.5 * x * (1.0 + jnp.tanh(np.sqrt(2.0 / np.pi).astype(np.float32) * (x + 0.044715 * (x * x * x))))


def lru_gates(c, wa_ref, ba, wx_ref, bx, lam):
    nb = c.shape[1] // B_BLOCK
    blocks = [c[:, i * B_BLOCK:(i + 1) * B_BLOCK] for i in range(nb)]
    gr = jnp.concatenate([_dot(blocks[i], wa_ref[i]) for i in range(nb)], axis=1) + ba
    gi = jnp.concatenate([_dot(blocks[i], wx_ref[i]) for i in range(nb)], axis=1) + bx
    log_a = -LRU_C * jax.nn.sigmoid(gr) * _softplus(-lam)
    a = jnp.exp(log_a)
    u = jnp.sqrt(-jnp.tanh(log_a) * (a * a + 1.0)) * jax.nn.sigmoid(gi) * c
    return a, u


def _lru_kernel(x_ref, g_ref, tail_ref, h0_ref, cw_ref, cb_ref, wa_ref, ba_ref, wx_ref, bx_ref, lam_ref, o_ref, h_ref, *,
                n_chunks, last_row):
    C, L = LRU_ROWS, LRU_L
    rows = lax.broadcasted_iota(jnp.int32, (C, L), 0)
    cw = cw_ref[...]

    def body(ci, carry):
        tail, h_prev, _ = carry
        r0 = pl.multiple_of(ci * C, C)
        x = x_ref[pl.ds(r0, C), :]
        xe = jnp.concatenate([tail, x], axis=0)
        c = cb_ref[...] + x * cw[3:4, :]
        for s in (1, 2, 3):
            c = c + pltpu.roll(xe, s, axis=0)[8:, :] * cw[3 - s:4 - s, :]
        a, u = lru_gates(c, wa_ref, ba_ref[...], wx_ref, bx_ref[...], lam_ref[...])
        d = 1
        while d < C:
            keep = rows >= d
            a_s = jnp.where(keep, pltpu.roll(a, d, axis=0), 1.0)
            u_s = jnp.where(keep, pltpu.roll(u, d, axis=0), 0.0)
            u = u + a * u_s
            a = a * a_s
            d *= 2
        h = a * h_prev + u
        o_ref[pl.ds(r0, C), :] = (h * _gelu_tanh(g_ref[pl.ds(r0, C), :])).astype(o_ref.dtype)
        return x[C - 8:, :], h[C - 1:C, :], h[last_row:last_row + 1, :]

    _, _, h_last = lax.fori_loop(0, n_chunks, body, (tail_ref[...], h0_ref[...], h0_ref[...]))
    h_ref[...] = h_last


def lru_mix(lru, tail0, h0, layer, conv_w, conv_b, w_a, b_a, w_x, b_x, lam, *, batch, seq, n_valid, out_rows):
    nj = B_WIDTH // LRU_L
    nb = LRU_L // B_BLOCK
    nc = seq // LRU_ROWS
    assert 0 <= n_valid - 1 - (nc - 1) * LRU_ROWS < LRU_ROWS
    vec = pl.BlockSpec((None, 1, LRU_L), lambda b, j: (layer, 0, j))
    wblk = pl.BlockSpec((None, nb, B_BLOCK, B_BLOCK), lambda b, j: (layer, j, 0, 0))
    return pl.pallas_call(
        functools.partial(_lru_kernel, n_chunks=nc, last_row=n_valid - 1 - (nc - 1) * LRU_ROWS),
        grid=(batch, nj),
        in_specs=[pl.BlockSpec((seq, LRU_L), lambda b, j: (b, j)),
                  pl.BlockSpec((seq, LRU_L), lambda b, j: (b, nj + j)),
                  pl.BlockSpec((None, 8, LRU_L), lambda b, j: (b, 0, j)),
                  pl.BlockSpec((None, 1, LRU_L), lambda b, j: (b, 0, j)),
                  pl.BlockSpec((None, 4, LRU_L), lambda b, j: (layer, 0, j)),
                  vec, wblk, vec, wblk, vec, vec],
        out_specs=[pl.BlockSpec((seq, LRU_L), lambda b, j: (b, j)),
                   pl.BlockSpec((None, 1, LRU_L), lambda b, j: (b, 0, j))],
        out_shape=[jax.ShapeDtypeStruct((out_rows, B_WIDTH), BF16),
                   jax.ShapeDtypeStruct((batch, 1, B_WIDTH), F32)],
        compiler_params=_cparams(("parallel", "parallel")),
        name="lru_mix",
    )(lru, lru, tail0, h0, conv_w, conv_b, w_a, b_a, w_x, b_x, lam)


C_BAND = 128
NEG_INF = float("-inf")


def _merge_groups(o_prev, lse_prev, o_new, lse_new):
    m = jnp.maximum(lse_prev, lse_new)
    wp = jnp.exp(lse_prev - m)
    wn = jnp.exp(lse_new - m)
    tot = wp + wn
    return (wp * o_prev + wn * o_new) / tot, m + jnp.log(tot)


def _attn_prompt_kernel(*refs, tq, has_prev_tile, first, last):
    if first:
        q_ref, k_ref, v_ref, kp_ref, vp_ref = refs[:5]
        rest = refs[5:]
    else:
        q_ref, k_ref, v_ref, kp_ref, vp_ref, oin_ref, lin_ref = refs[:7]
        rest = refs[7:]
    o_ref = rest[0]
    l_ref = None if last else rest[1]
    B = C_BAND
    iq = lax.broadcasted_iota(jnp.int32, (B, 2 * B), 0)
    jk = lax.broadcasted_iota(jnp.int32, (B, 2 * B), 1)
    band_ok = (jk >= iq) & (jk <= iq + B)
    cur_only = band_ok & (jk >= B)
    first_key = jnp.where(pl.program_id(2) == 0, B, 0)
    for h in range(C_HEADS):
        ls = slice(h * C_HEAD_DIM, (h + 1) * C_HEAD_DIM)
        for s in range(tq // B):
            rs = slice(s * B, (s + 1) * B)
            q = q_ref[rs, ls]
            if s == 0:
                kcat = jnp.concatenate([kp_ref[:, ls], k_ref[rs, ls]], axis=0)
                vcat = jnp.concatenate([vp_ref[:, ls], v_ref[rs, ls]], axis=0)
                mask = (band_ok & (jk >= first_key)) if has_prev_tile else cur_only
            else:
                kcat = k_ref[(s - 1) * B:(s + 1) * B, ls]
                vcat = v_ref[(s - 1) * B:(s + 1) * B, ls]
                mask = band_ok
            sc = jnp.where(mask, _dot_nt(q, kcat) * C_SCALE, NEG_INF)
            m = jnp.max(sc, axis=-1, keepdims=True)
            p = jnp.exp(sc - m)
            den = jnp.sum(p, axis=-1, keepdims=True)
            o = _dot(p, vcat) / den
            lse = jnp.broadcast_to(m + jnp.log(den), o.shape)
            if not first:
                o, lse = _merge_groups(oin_ref[rs, ls], lin_ref[rs, ls], o, lse)
            o_ref[rs, ls] = o.astype(o_ref.dtype)
            if not last:
                l_ref[rs, ls] = lse


def attn_prompt_group(qn, kn, v, o_run, lse_run, *, gi, dil, batch, seq, first, last):
    rows = qn.shape[0]
    L = seq // dil
    tq = min(L, 512)
    nt = L // tq
    W = C_GROUP_WIDTH
    view = lambda t: t.reshape(rows // dil, dil * t.shape[1])
    qmap = lambda b, r, i: (b * nt + i, r * 3 + gi)
    pmap = lambda b, r, i: (jnp.maximum((b * L + i * tq) // C_BAND - 1, 0), r * 3 + gi)
    omap = lambda b, r, i: (b * nt + i, r)
    in_specs = [pl.BlockSpec((tq, W), qmap), pl.BlockSpec((tq, W), qmap), pl.BlockSpec((tq, W), qmap),
                pl.BlockSpec((C_BAND, W), pmap), pl.BlockSpec((C_BAND, W), pmap)]
    args = [view(qn), view(kn), view(v), view(kn), view(v)]
    if not first:
        in_specs += [pl.BlockSpec((tq, W), omap), pl.BlockSpec((tq, W), omap)]
        args += [view(o_run), view(lse_run)]
    out_dt = BF16 if last else F32
    out_specs = [pl.BlockSpec((tq, W), omap)]
    out_shape = [jax.ShapeDtypeStruct((rows // dil, dil * W), out_dt)]
    if not last:
        out_specs.append(pl.BlockSpec((tq, W), omap))
        out_shape.append(jax.ShapeDtypeStruct((rows // dil, dil * W), F32))
    res = pl.pallas_call(
        functools.partial(_attn_prompt_kernel, tq=tq, has_prev_tile=nt > 1, first=first, last=last),
        grid=(batch, dil, nt),
        in_specs=in_specs, out_specs=out_specs, out_shape=out_shape,
        compiler_params=_cparams(("parallel", "parallel", "parallel")),
        name="attn_prompt_g%d" % gi,
    )(*args)
    return [t.reshape(rows, W) for t in res]


def _attn_sample_kernel(*refs, lb, dil, n_new, first):
    if first:
        q_ref, kn_ref, vn_ref, kc_ref, vc_ref, o_ref, l_ref = refs
    else:
        q_ref, kn_ref, vn_ref, kc_ref, vc_ref, oin_ref, lin_ref, o_ref, l_ref = refs
    R = q_ref.shape[0]
    qi = lax.broadcasted_iota(jnp.int32, (R, lb), 0)
    ci = lax.broadcasted_iota(jnp.int32, (R, lb), 1)
    off = lb + qi - ci
    cache_ok = (off % dil == 0) & (off <= C_BAND * dil)
    qn_i = lax.broadcasted_iota(jnp.int32, (R, R), 0)
    nn_i = lax.broadcasted_iota(jnp.int32, (R, R), 1)
    new_ok = (nn_i <= qn_i) & ((qn_i - nn_i) % dil == 0) & (nn_i < n_new)
    for h in range(C_HEADS):
        ls = slice(h * C_HEAD_DIM, (h + 1) * C_HEAD_DIM)
        q = q_ref[:, ls]
        sc = jnp.where(cache_ok, _dot_nt(q, kc_ref[:, ls]) * C_SCALE, NEG_INF)
        sn = jnp.where(new_ok, _dot_nt(q, kn_ref[:, ls]) * C_SCALE, NEG_INF)
        m = jnp.maximum(jnp.max(sc, axis=-1, keepdims=True), jnp.max(sn, axis=-1, keepdims=True))
        pc = jnp.exp(sc - m)
        pn = jnp.exp(sn - m)
        den = jnp.sum(pc, axis=-1, keepdims=True) + jnp.sum(pn, axis=-1, keepdims=True)
        o = (_dot(pc, vc_ref[:, ls]) + _dot(pn, vn_ref[:, ls])) / den
        lse = jnp.broadcast_to(m + jnp.log(den), o.shape)
        if not first:
            o, lse = _merge_groups(oin_ref[:, ls], lin_ref[:, ls], o, lse)
        o_ref[:, ls] = o
        l_ref[:, ls] = lse


def attn_sample_group(qn, kn, v, k_cache, v_cache, o_run, lse_run, *, layer, gi, dil, n_new, first):
    R = 8
    batch = qn.shape[0] // R
    lb = k_cache.shape[2]
    W = C_GROUP_WIDTH
    qmap = lambda b: (b, gi)
    omap = lambda b: (b, 0)
    cmap = lambda b: (layer, b, 0, 0)
    in_specs = [pl.BlockSpec((R, W), qmap)] * 3 + [pl.BlockSpec((None, None, lb, W), cmap)] * 2
    args = [qn, kn, v, k_cache, v_cache]
    if not first:
        in_specs += [pl.BlockSpec((R, W), omap)] * 2
        args += [o_run, lse_run]
    return pl.pallas_call(
        functools.partial(_attn_sample_kernel, lb=lb, dil=dil, n_new=n_new, first=first),
        grid=(batch,),
        in_specs=in_specs,
        out_specs=[pl.BlockSpec((R, W), omap)] * 2,
        out_shape=[jax.ShapeDtypeStruct((batch * R, W), F32)] * 2,
        compiler_params=_cparams(("parallel",)),
        name="attn_sample_g%d" % gi,
    )(*args)


def _cast(w, rows=None, cols=None):
    w = w.astype(BF16)
    pr = 0 if rows is None else rows - w.shape[1]
    pc = 0 if cols is None else cols - w.shape[2]
    if pr or pc:
        w = jnp.pad(w, ((0, 0), (0, pr), (0, pc)))
    return w


_ZW0, _ZA0, _ZG0 = A_SPLIT_OFFSETS[2], A_SPLIT_OFFSETS[3], A_SPLIT_OFFSETS[4]


def _pa_layout(x):
    z = jnp.zeros(x.shape[:-1] + (LANE - A_DECAY_RANK,), x.dtype)
    return jnp.concatenate([x[..., :_ZW0], x[..., _ZW0:_ZA0], z, x[..., _ZA0:_ZG0], z, x[..., _ZG0:]], axis=-1)


def _pa_unlayout(x):
    return jnp.concatenate([x[..., :_ZA0], x[..., _ZW0 + LANE:_ZW0 + LANE + A_ICLR_RANK], x[..., _ZW0 + 2 * LANE:]],
                           axis=-1)


def _sample_rows(t, rows_per_batch):
    s = t[N_PROMPT:N_PROMPT + N_SAMPLE].reshape(DEC_BATCH, DEC_SEQ, t.shape[1])
    return jnp.pad(s, ((0, 0), (0, rows_per_batch - DEC_SEQ), (0, 0))).reshape(DEC_BATCH * rows_per_batch, t.shape[1])


def _with_tail(full, sample_out, rows_per_batch):
    w = full.shape[1]
    s = sample_out.reshape(DEC_BATCH, rows_per_batch, w)[:, :DEC_SEQ].reshape(N_SAMPLE, w).astype(full.dtype)
    tail = jnp.pad(s, ((0, M_PAD - N_PROMPT - N_SAMPLE), (0, 0)))
    return lax.dynamic_update_slice(full, tail, (N_PROMPT, 0))


def _state_to_blockdiag(s):
    b = s.shape[0]
    s5 = s.reshape(b, A_HEADS // 4, 4, A_HEAD_SIZE, A_HEAD_SIZE)
    return jnp.einsum('bgivk,ij->bgivjk', s5, jnp.eye(4, dtype=s.dtype)).reshape(b, A_HEADS // 4, RW_L, RW_L)


def _blockdiag_to_state(s):
    b = s.shape[0]
    s6 = s.reshape(b, A_HEADS // 4, 4, A_HEAD_SIZE, 4, A_HEAD_SIZE)
    return jnp.einsum('bgivik->bgivk', s6).reshape(b, A_HEADS, A_HEAD_SIZE, A_HEAD_SIZE)


def kernel(x_prompt, x_sample, state_rwkv_shift, state_rwkv_wkv, state_lru_conv, state_lru_h, cache_dil1_k, cache_dil1_v, cache_dil2_k, cache_dil2_v, cache_dil3_k, cache_dil3_v, norm_ffn1, ffn1_w_gate, ffn1_w_up, ffn1_w_down, norm_mix, w_in, rwkv_mu, rwkv_w0, rwkv_w_up, rwkv_a0, rwkv_a_up, rwkv_g_up, rwkv_k_k, rwkv_k_a, rwkv_r_k, rwkv_ln_w, rwkv_ln_b, lru_conv_w, lru_conv_b, lru_w_a, lru_b_a, lru_w_x, lru_b_x, lru_lambda, attn_q_norm, attn_k_norm, w_br_a, w_br_b, w_br_c, w_out, norm_ffn2, ffn2_w_gate, ffn2_w_up, ffn2_w_down):
    L = DEPTH
    caches = ((cache_dil1_k, cache_dil1_v), (cache_dil2_k, cache_dil2_v), (cache_dil3_k, cache_dil3_v))

    wg1, wu1 = _cast(ffn1_w_gate, cols=D_FF_PAD), _cast(ffn1_w_up, cols=D_FF_PAD)
    wd1 = _cast(ffn1_w_down, rows=D_FF_PAD)
    wg2, wu2 = _cast(ffn2_w_gate, cols=D_FF_PAD), _cast(ffn2_w_up, cols=D_FF_PAD)
    wd2 = _cast(ffn2_w_down, rows=D_FF_PAD)
    o = IN_OFFSETS
    w_pa = _pa_layout(w_in[:, :, :o[0]]).astype(BF16)
    w_lru = w_in[:, :, o[0]:o[2]].astype(BF16)
    w_q, w_k, w_v = (w_in[:, :, o[i]:o[i + 1]].astype(BF16) for i in (2, 3, 4))
    w_gates = w_in[:, :, o[5]:].astype(BF16)
    wba, wbb, wbc, wo = (w.astype(BF16) for w in (w_br_a, w_br_b, w_br_c, w_out))

    vec = lambda t: t.reshape(L, 1, -1)
    g1, gm, g2 = vec(norm_ffn1), vec(norm_mix), vec(norm_ffn2)
    rw_params = (vec(_pa_layout(rwkv_mu)), vec(rwkv_w0), _cast(rwkv_w_up, rows=LANE).astype(F32), vec(rwkv_a0),
                 _cast(rwkv_a_up, rows=LANE).astype(F32), rwkv_g_up, vec(rwkv_k_k), vec(rwkv_k_a), vec(rwkv_r_k),
                 vec(rwkv_ln_w), vec(rwkv_ln_b))
    lru_params = (lru_conv_w, vec(lru_conv_b), lru_w_a, vec(lru_b_a), lru_w_x, vec(lru_b_x), vec(lru_lambda))
    q_gain = attn_q_norm.reshape(L, C_N_GROUPS, 1, C_HEAD_DIM)
    k_gain = attn_k_norm.reshape(L, C_N_GROUPS, 1, C_HEAD_DIM)

    shift_p0 = jnp.zeros((BATCH, 1, PA_W), F32)
    wkv_p0 = jnp.zeros((BATCH, A_WIDTH // RW_L, RW_L, RW_L), F32)
    tail_p0 = jnp.zeros((BATCH, 8, B_WIDTH), F32)
    h_p0 = jnp.zeros((BATCH, 1, B_WIDTH), F32)

    x = jnp.concatenate([x_prompt.reshape(N_PROMPT, D_MODEL), x_sample.reshape(N_SAMPLE, D_MODEL),
                         jnp.zeros((M_PAD - N_PROMPT - N_SAMPLE, D_MODEL), F32)], axis=0)

    outs_p, outs_s = [], []
    for l in range(L):
        u = rmsnorm_rows(x, g1, l)
        hff = ffn_gateup(u, wg1, wu1, l)
        h = matmul_resid(hff, wd1, x, l, 1024, 2816, 0.5, "ffn1_down")
        u = rmsnorm_rows(h, gm, l)
        pa = matmul_fullk(u, w_pa, l, 512, F32, "proj_rwkv")
        lru = matmul_fullk(u, w_lru, l, 1024, F32, "proj_lru")
        qn = matmul_qknorm(u, w_q, q_gain, l, "proj_q")
        kn = matmul_qknorm(u, w_k, k_gain, l, "proj_k")
        vv = matmul_fullk(u, w_v, l, 1024, F32, "proj_v")
        gates = matmul_fullk(u, w_gates, l, 1024, F32, "proj_gates")

        oa, wkv_p = rwkv_mix(pa, shift_p0, wkv_p0, l, *rw_params, batch=BATCH, seq=SEQ, n_valid=SEQ, out_rows=M_PAD)
        oa_s, wkv_s = rwkv_mix(_sample_rows(pa, RW_C), _pa_layout(state_rwkv_shift[l])[:, None, :],
                               _state_to_blockdiag(state_rwkv_wkv[l]), l, *rw_params, batch=DEC_BATCH, seq=RW_C,
                               n_valid=DEC_SEQ, out_rows=DEC_BATCH * RW_C)
        oa = _with_tail(oa, oa_s, RW_C)
        shift_p = pa[:N_PROMPT].reshape(BATCH, SEQ, PA_W)[:, -1]
        shift_s = pa[N_PROMPT:N_PROMPT + N_SAMPLE].reshape(DEC_BATCH, DEC_SEQ, PA_W)[:, -1]

        ob, lh_p = lru_mix(lru, tail_p0, h_p0, l, *lru_params, batch=BATCH, seq=SEQ, n_valid=SEQ, out_rows=M_PAD)
        tail_s = jnp.pad(state_lru_conv[l], ((0, 0), (8 - (CONV_WIDTH - 1), 0), (0, 0)))
        ob_s, lh_s = lru_mix(_sample_rows(lru, LRU_ROWS), tail_s, state_lru_h[l][:, None, :], l, *lru_params,
                             batch=DEC_BATCH, seq=LRU_ROWS, n_valid=DEC_SEQ, out_rows=DEC_BATCH * LRU_ROWS)
        ob = _with_tail(ob, ob_s, LRU_ROWS)
        lx_p = lru[:N_PROMPT, :B_WIDTH].reshape(BATCH, SEQ, B_WIDTH)
        lx_s = lru[N_PROMPT:N_PROMPT + N_SAMPLE, :B_WIDTH].reshape(DEC_BATCH, DEC_SEQ, B_WIDTH)
        conv_p = lx_p[:, SEQ - (CONV_WIDTH - 1):]
        conv_s = jnp.concatenate([state_lru_conv[l], lx_s], axis=1)[:, -(CONV_WIDTH - 1):]

        o_run = lse_run = None
        for gi, (_, dil) in enumerate(C_GROUPS):
            res = attn_prompt_group(qn, kn, vv, o_run, lse_run, gi=gi, dil=dil, batch=BATCH, seq=SEQ,
                                    first=gi == 0, last=gi == C_N_GROUPS - 1)
            if gi < C_N_GROUPS - 1:
                o_run, lse_run = res
        oc = res[0]
        qs, ks, vs = _sample_rows(qn, 8), _sample_rows(kn, 8), _sample_rows(vv, 8)
        o_run = lse_run = None
        for gi, (window, dil) in enumerate(C_GROUPS):
            kc, vc = (c.reshape(L, DEC_BATCH, window, C_GROUP_WIDTH) for c in caches[gi])
            o_run, lse_run = attn_sample_group(qs, ks, vs, kc, vc, o_run, lse_run, layer=l, gi=gi, dil=dil,
                                               n_new=DEC_SEQ, first=gi == 0)
        oc = _with_tail(oc, o_run, 8)

        new_p = [_pa_unlayout(shift_p), _blockdiag_to_state(wkv_p), conv_p, lh_p[:, 0]]
        new_s = [_pa_unlayout(shift_s), _blockdiag_to_state(wkv_s), conv_s, lh_s[:, 0]]
        shp_p = (BATCH, SEQ, C_N_GROUPS, C_HEADS, C_HEAD_DIM)
        shp_s = (DEC_BATCH, DEC_SEQ, C_N_GROUPS, C_HEADS, C_HEAD_DIM)
        for gi, (window, _) in enumerate(C_GROUPS):
            keep = min(window, SEQ)
            for t, cache in ((kn, caches[gi][0]), (vv, caches[gi][1])):
                new_p.append(t[:N_PROMPT].reshape(shp_p)[:, SEQ - keep:, gi])
                fresh = t[N_PROMPT:N_PROMPT + N_SAMPLE].reshape(shp_s)[:, :, gi]
                new_s.append(jnp.concatenate([cache[l][:, DEC_SEQ:], fresh], axis=1))
        outs_p.append(new_p)
        outs_s.append(new_s)

        merged = branch_merge(oa, ob, oc, wba, wbb, wbc, gates, l)
        h = matmul_resid(merged, wo, h, l, 512, D_MODEL, 1.0, "w_out")
        u = rmsnorm_rows(h, g2, l)
        hff = ffn_gateup(u, wg2, wu2, l)
        x = matmul_resid(hff, wd2, h, l, 1024, 2816, 0.5, "ffn2_down")

    res = [x[:N_PROMPT].reshape(BATCH, SEQ, D_MODEL),
           x[N_PROMPT:N_PROMPT + N_SAMPLE].reshape(DEC_BATCH, DEC_SEQ, D_MODEL)]
    for i in range(10):
        res.append(jnp.stack([o_[i] for o_ in outs_p]))
        res.append(jnp.stack([o_[i] for o_ in outs_s]))
    return tuple(res)
```

```python
import functools

import numpy as np
import jax
import jax.numpy as jnp
from jax import lax
from jax.experimental import pallas as pl
from jax.experimental.pallas import tpu as pltpu

F32 = jnp.float32
BF16 = jnp.bfloat16

D_MODEL = 4096
BATCH = 4
SEQ = 2048
DEPTH = 4
DEC_BATCH = 8
DEC_SEQ = 4
PAST_LEN = 8192

A_HEAD_SIZE = 64
A_WIDTH = D_MODEL // 2
A_HEADS = A_WIDTH // A_HEAD_SIZE
A_DECAY_RANK = 96
A_ICLR_RANK = 96
A_GATE_RANK = 256
A_GN_EPS = 64e-5
A_SHIFT_WIDTH = 3 * A_WIDTH + A_DECAY_RANK + A_ICLR_RANK + A_GATE_RANK
A_SPLIT_OFFSETS = (A_WIDTH, 2 * A_WIDTH, 3 * A_WIDTH, 3 * A_WIDTH + A_DECAY_RANK,
                   3 * A_WIDTH + A_DECAY_RANK + A_ICLR_RANK)
B_WIDTH = D_MODEL // 2
B_BLOCKS = 16
B_BLOCK = B_WIDTH // B_BLOCKS
CONV_WIDTH = 4
LRU_C = 8.0
C_HEADS = 8
C_HEAD_DIM = 128
C_GROUPS = ((128, 1), (512, 4), (2048, 16))
C_N_GROUPS = 3
C_GROUP_WIDTH = C_HEADS * C_HEAD_DIM
C_QKV_WIDTH = C_N_GROUPS * C_GROUP_WIDTH
C_SCALE = C_HEAD_DIM ** -0.5
QK_EPS = 1e-6
D_FF = 11008
NORM_EPS = 1e-6
IN_SPLITS = (A_SHIFT_WIDTH, B_WIDTH, B_WIDTH, C_QKV_WIDTH, C_QKV_WIDTH, C_QKV_WIDTH, D_MODEL, D_MODEL, D_MODEL)
IN_OFFSETS = tuple(int(s) for s in np.cumsum(IN_SPLITS)[:-1])

LANE = 128
VMEM_LIMIT_BYTES = 56 * 1024 * 1024

N_PROMPT = BATCH * SEQ
N_SAMPLE = DEC_BATCH * DEC_SEQ
ROW_TILE = 1040
M_PAD = 8 * ROW_TILE
A_SHIFT_PAD = 13 * 512

assert M_PAD >= N_PROMPT + N_SAMPLE


def _cparams(sem):
    return pltpu.CompilerParams(dimension_semantics=sem, vmem_limit_bytes=VMEM_LIMIT_BYTES)


def _rmsnorm_kernel(x_ref, g_ref, o_ref):
    x = x_ref[...]
    ms = jnp.mean(x * x, axis=-1, keepdims=True)
    o_ref[...] = (x * lax.rsqrt(ms + NORM_EPS) * g_ref[...]).astype(o_ref.dtype)


def rmsnorm_rows(x, g, layer):
    m, d = x.shape
    tm = 208
    return pl.pallas_call(
        _rmsnorm_kernel,
        grid=(m // tm,),
        in_specs=[pl.BlockSpec((tm, d), lambda i: (i, 0)),
                  pl.BlockSpec((None, 1, d), lambda i: (layer, 0, 0))],
        out_specs=pl.BlockSpec((tm, d), lambda i: (i, 0)),
        out_shape=jax.ShapeDtypeStruct((m, d), BF16),
        compiler_params=_cparams(("parallel",)),
        name="rmsnorm",
    )(x, g)


def _mm_kernel(x_ref, w_ref, o_ref):
    o_ref[...] = jnp.dot(x_ref[...], w_ref[...].astype(BF16), preferred_element_type=F32).astype(o_ref.dtype)


def matmul_fullk(x, w, layer, tn, out_dtype, name, col0=0, n=None):
    m, k = x.shape
    n = w.shape[2] if n is None else n
    c0 = col0 // tn
    assert c0 * tn == col0 and n % tn == 0
    return pl.pallas_call(
        _mm_kernel,
        grid=(m // ROW_TILE, n // tn),
        in_specs=[pl.BlockSpec((ROW_TILE, k), lambda i, j: (i, 0)),
                  pl.BlockSpec((None, k, tn), lambda i, j: (layer, 0, c0 + j))],
        out_specs=pl.BlockSpec((ROW_TILE, tn), lambda i, j: (i, j)),
        out_shape=jax.ShapeDtypeStruct((m, n), out_dtype),
        compiler_params=_cparams(("parallel", "parallel")),
        name=name,
    )(x, w)


def _qknorm_mm_kernel(x_ref, w_ref, g_ref, o_ref):
    acc = jnp.dot(x_ref[...], w_ref[...], preferred_element_type=F32)
    gain = g_ref[...]
    for h in range(acc.shape[1] // C_HEAD_DIM):
        ls = slice(h * C_HEAD_DIM, (h + 1) * C_HEAD_DIM)
        t = acc[:, ls]
        o_ref[:, ls] = t * lax.rsqrt(jnp.mean(t * t, axis=-1, keepdims=True) + QK_EPS) * gain


def matmul_qknorm(x, w, gain, layer, name, col0):
    m, k = x.shape
    n = C_QKV_WIDTH
    tn = C_GROUP_WIDTH
    c0 = col0 // tn
    assert c0 * tn == col0
    return pl.pallas_call(
        _qknorm_mm_kernel,
        grid=(m // ROW_TILE, n // tn),
        in_specs=[pl.BlockSpec((ROW_TILE, k), lambda i, j: (i, 0)),
                  pl.BlockSpec((None, k, tn), lambda i, j: (layer, 0, c0 + j)),
                  pl.BlockSpec((None, None, 1, C_HEAD_DIM), lambda i, j: (layer, j, 0, 0))],
        out_specs=pl.BlockSpec((ROW_TILE, tn), lambda i, j: (i, j)),
        out_shape=jax.ShapeDtypeStruct((m, n), F32),
        compiler_params=_cparams(("parallel", "parallel")),
        name=name,
    )(x, w, gain)


def _gateup_kernel(x_ref, wg_ref, wu_ref, o_ref):
    x = x_ref[...]
    g = jnp.dot(x, wg_ref[...].astype(BF16), preferred_element_type=F32)
    u = jnp.dot(x, wu_ref[...].astype(BF16), preferred_element_type=F32)
    o_ref[...] = (g * jax.nn.sigmoid(g) * u).astype(o_ref.dtype)


def ffn_gateup(x, wg, wu, layer):
    m, k = x.shape
    n = wg.shape[2]
    tn = 512
    return pl.pallas_call(
        _gateup_kernel,
        grid=(m // ROW_TILE, pl.cdiv(n, tn)),
        in_specs=[pl.BlockSpec((ROW_TILE, k), lambda i, j: (i, 0), pipeline_mode=pl.Buffered(1)),
                  pl.BlockSpec((None, k, tn), lambda i, j: (layer, 0, j)),
                  pl.BlockSpec((None, k, tn), lambda i, j: (layer, 0, j))],
        out_specs=pl.BlockSpec((ROW_TILE, tn), lambda i, j: (i, j)),
        out_shape=jax.ShapeDtypeStruct((m, n), BF16),
        compiler_params=_cparams(("parallel", "parallel")),
        name="ffn_gateup",
    )(x, wg, wu)


def _mm_resid_kernel(x_ref, w_ref, r_ref, o_ref, *, scale, k_total):
    kk = pl.program_id(2)
    tk = x_ref.shape[1]
    n_k = pl.cdiv(k_total, tk)

    def accumulate(x, w):
        p = scale * jnp.dot(x, w, preferred_element_type=F32)

        @pl.when(kk == 0)
        def _():
            o_ref[...] = r_ref[...] + p

        @pl.when(kk > 0)
        def _():
            o_ref[...] += p

    if k_total % tk == 0:
        accumulate(x_ref[...], w_ref[...])
    else:
        @pl.when(kk < n_k - 1)
        def _():
            accumulate(x_ref[...], w_ref[...])

        @pl.when(kk == n_k - 1)
        def _():
            valid = k_total - (n_k - 1) * tk
            x, w = x_ref[...], w_ref[...]
            x = jnp.where(lax.broadcasted_iota(jnp.int32, x.shape, 1) < valid, x, jnp.zeros((), x.dtype))
            w = jnp.where(lax.broadcasted_iota(jnp.int32, w.shape, 0) < valid, w, jnp.zeros((), w.dtype))
            accumulate(x, w)


def matmul_resid(x, w, resid, layer, tn, tk, scale, name):
    m, k = x.shape
    n = w.shape[2]
    return pl.pallas_call(
        functools.partial(_mm_resid_kernel, scale=scale, k_total=k),
        grid=(m // ROW_TILE, n // tn, pl.cdiv(k, tk)),
        in_specs=[pl.BlockSpec((ROW_TILE, tk), lambda i, j, kk: (i, kk)),
                  pl.BlockSpec((None, tk, tn), lambda i, j, kk: (layer, kk, j)),
                  pl.BlockSpec((ROW_TILE, tn), lambda i, j, kk: (i, j))],
        out_specs=pl.BlockSpec((ROW_TILE, tn), lambda i, j, kk: (i, j)),
        out_shape=jax.ShapeDtypeStruct((m, n), F32),
        compiler_params=_cparams(("parallel", "parallel", "arbitrary")),
        name=name,
    )(x, w, resid)


def _merge_kernel(oa_ref, ob_ref, oc_ref, wa_ref, wb_ref, wc_ref, ga_ref, gb_ref, gc_ref, o_ref):
    a = jnp.dot(oa_ref[...], wa_ref[...], preferred_element_type=F32)
    b = jnp.dot(ob_ref[...], wb_ref[...], preferred_element_type=F32)
    c = jnp.dot(oc_ref[...], wc_ref[...], preferred_element_type=F32)
    o_ref[...] = (jax.nn.sigmoid(ga_ref[...]) * a + jax.nn.sigmoid(gb_ref[...]) * b
                  + jax.nn.sigmoid(gc_ref[...]) * c).astype(o_ref.dtype)


def branch_merge(oa, ob, oc, wa, wb, wc, gates, layer):
    m = oa.shape[0]
    n = wa.shape[2]
    tn = 512
    nb = n // tn
    row = lambda i, j: (i, 0)
    wcol = lambda i, j: (layer, 0, j)
    return pl.pallas_call(
        _merge_kernel,
        grid=(m // ROW_TILE, nb),
        in_specs=[pl.BlockSpec((ROW_TILE, oa.shape[1]), row),
                  pl.BlockSpec((ROW_TILE, ob.shape[1]), row),
                  pl.BlockSpec((ROW_TILE, oc.shape[1]), row),
                  pl.BlockSpec((None, wa.shape[1], tn), wcol),
                  pl.BlockSpec((None, wb.shape[1], tn), wcol),
                  pl.BlockSpec((None, wc.shape[1], tn), wcol),
                  pl.BlockSpec((ROW_TILE, tn), lambda i, j: (i, j)),
                  pl.BlockSpec((ROW_TILE, tn), lambda i, j: (i, nb + j)),
                  pl.BlockSpec((ROW_TILE, tn), lambda i, j: (i, 2 * nb + j))],
        out_specs=pl.BlockSpec((ROW_TILE, tn), lambda i, j: (i, j)),
        out_shape=jax.ShapeDtypeStruct((m, n), BF16),
        compiler_params=_cparams(("parallel", "parallel")),
        name="branch_merge",
    )(oa, ob, oc, wa, wb, wc, gates, gates, gates)


RW_C = 64
RW_L = 256
RW_HS = A_HEAD_SIZE
RW_UNROLL = 8
PA_W = A_SHIFT_PAD


def _dot(a, b):
    return jnp.dot(a.astype(BF16), b.astype(BF16), preferred_element_type=F32)


def _dot_nt(a, b):
    return lax.dot_general(a.astype(BF16), b.astype(BF16), (((1,), (1,)), ((), ())), preferred_element_type=F32)


def _dot_tn(a, b):
    return lax.dot_general(a.astype(BF16), b.astype(BF16), (((0,), (0,)), ((), ())), preferred_element_type=F32)


def _split_dot(e, x, parts):
    acc = None
    rem = x
    for _ in range(parts):
        p = rem.astype(BF16)
        rem = rem - p.astype(F32)
        d = jnp.dot(e, p, preferred_element_type=F32)
        acc = d if acc is None else acc + d
    return acc


def _split_dot_r(x, e, parts):
    acc = None
    rem = x
    for _ in range(parts):
        p = rem.astype(BF16)
        rem = rem - p.astype(F32)
        d = jnp.dot(p, e, preferred_element_type=F32)
        acc = d if acc is None else acc + d
    return acc


def _softplus(x):
    return jnp.maximum(x, 0.0) + jnp.log1p(jnp.exp(-jnp.abs(x)))


def _interleave(gens):
    results = [None] * len(gens)
    live = list(range(len(gens)))
    while live:
        for i in list(live):
            try:
                next(gens[i])
            except StopIteration as done:
                results[i] = done.value
                live.remove(i)
    return results


def rwkv_prep(r, k, v, z1, z2, w0, w_up, a0, a_up, g_up1, g_up2, k_k, k_a, r_k, seg):
    logw = -_softplus(-(w0 + _dot(jnp.tanh(z1), w_up))) - 0.5
    ld = -jnp.exp(logw)
    a = jax.nn.sigmoid(a0 + _dot(z1, a_up))
    g = _dot(jax.nn.sigmoid(z1), g_up1) + _dot(jax.nn.sigmoid(z2), g_up2)
    kk = k * k_k
    ss = _split_dot_r(kk * kk, seg, 2)
    yield
    kk = kk / jnp.maximum(jnp.sqrt(ss), 1e-12)
    k_h = k * (1.0 + (a - 1.0) * k_a)
    bonus = _split_dot_r(r * k_h * r_k, seg, 2) * v
    yield
    return ld, -kk, kk * a, k_h, g, bonus


def _rwkv_kernel(r_ref, k_ref, v_ref, wa_ref, zg_ref, sh_r, sh_k, sh_v, sh_wa, sh_g, s0_ref,
                 mu_r, mu_k, mu_v, mu_wa, mu_g,
                 w0_ref, wup_ref, a0_ref, aup_ref, gup1_ref, gup2_ref, kk_ref, ka_ref, rk_ref, lnw_ref, lnb_ref,
                 o_ref, s_ref,
                 at_s, rh_s, nrb_s, gm_s, hm_s, uv_s, yv_s, bon_s, g_s, gc_s, *, n_chunks, n_valid, unroll):
    C, L = RW_C, RW_L
    row = lax.broadcasted_iota(jnp.int32, (C, 4 * C), 0)
    col = lax.broadcasted_iota(jnp.int32, (C, 4 * C), 1) % C
    strict = col < row
    incl = col <= row
    eye_cat = (col == row).astype(F32)
    bd_mask = (lax.broadcasted_iota(jnp.int32, (L, L), 0) // RW_HS) == (lax.broadcasted_iota(jnp.int32, (L, L), 1) // RW_HS)
    seg = bd_mask.astype(BF16)
    ltri = (lax.broadcasted_iota(jnp.int32, (C, C), 1) <= lax.broadcasted_iota(jnp.int32, (C, C), 0)).astype(BF16)
    first_row = lax.broadcasted_iota(jnp.int32, (C, L), 0) == 0

    def stack4(z):
        zb = z.astype(BF16)
        return jnp.where(bd_mask, jnp.concatenate([zb, zb, zb, zb], axis=0), jnp.zeros((), BF16))

    def shifted(x_ref, mu_ref, r0, last):
        x = x_ref[pl.ds(r0, C), :]
        prev = jnp.where(first_row, last, pltpu.roll(x, 1, axis=0))
        return x + (prev - x) * mu_ref[...], x[C - 1:C, :]

    def chunk_terms(xr, xk, xv, xwa, xg):
        ld, a_vec, b_vec, k_h, g, bonus = yield from rwkv_prep(
            xr, xk, xv, xwa, xg, w0_ref[...], wup_ref[...], a0_ref[...], aup_ref[...],
            gup1_ref[...], gup2_ref[...], kk_ref[...], ka_ref[...], rk_ref[...], seg)
        if n_valid < n_chunks * C:
            live = lax.broadcasted_iota(jnp.int32, (C, L), 0) < n_valid
            ld, a_vec, b_vec, k_h = (jnp.where(live, t_, 0.0) for t_ in (ld, a_vec, b_vec, k_h))
        cum = _split_dot(ltri, ld, 3)
        yield
        cum_c = cum[C - 1:C, :]
        ah = a_vec * jnp.exp(cum - ld)
        rh = xr * jnp.exp(cum)
        inv = jnp.exp(-cum)
        rel = jnp.exp(cum_c - cum)
        ar = jnp.concatenate([ah, rh], axis=0)
        pb = _dot_nt(ar, stack4(b_vec * inv))
        pk = _dot_nt(ar, stack4(k_h * inv))
        yield
        nab = jnp.where(strict, pb[:C], 0.0)
        nrb = jnp.where(incl, pb[C:], 0.0)
        nak = jnp.where(strict, pk[:C], 0.0)
        nrk = jnp.where(incl, pk[C:], 0.0)
        t = eye_cat + nab
        n = _dot(nab, stack4(nab))
        nakv = _dot(nak, stack4(xv))
        yv = _dot(nrk, stack4(xv))
        yield
        for _ in range(5):
            p = _dot(jnp.concatenate([n, t], axis=0), stack4(n))
            yield
            t = t + p[C:]
            n = p[:C]
        at = _dot(t, stack4(ah))
        uv = _dot(t, stack4(nakv))
        yield
        bt = (b_vec * rel).astype(BF16)
        kt = (k_h * rel).astype(BF16)
        gm = _dot_tn(at, bt)
        hm = _dot_tn(jnp.concatenate([uv.astype(BF16), xv.astype(BF16)], axis=0), jnp.concatenate([bt, kt], axis=0))
        yield
        return (at.astype(BF16), rh.astype(BF16), nrb.astype(BF16), jnp.where(bd_mask, gm, 0.0).astype(BF16),
                jnp.where(bd_mask, hm, 0.0), uv, yv, bonus, g, jnp.broadcast_to(jnp.exp(cum_c), (8, L)))

    scratch = (at_s, rh_s, nrb_s, gm_s, hm_s, uv_s, yv_s, bon_s, g_s, gc_s)
    in_refs = ((r_ref, mu_r), (k_ref, mu_k), (v_ref, mu_v), (wa_ref, mu_wa), (zg_ref, mu_g))

    def phase1(i, carry):
        gens = []
        for j in range(unroll):
            r0 = pl.multiple_of((i * unroll + j) * C, C)
            xs = []
            nxt = []
            for (x_ref, mu_ref), last in zip(in_refs, carry):
                xm, tail = shifted(x_ref, mu_ref, r0, last)
                xs.append(xm)
                nxt.append(tail)
            carry = tuple(nxt)
            gens.append(chunk_terms(*xs))
        for j, vals in enumerate(_interleave(gens)):
            for ref, val in zip(scratch, vals):
                ref[i * unroll + j] = val
        return carry

    lax.fori_loop(0, n_chunks // unroll, phase1, (sh_r[...], sh_k[...], sh_v[...], sh_wa[...], sh_g[...]))

    def chunk_out(c, sb):
        u = _dot_nt(at_s[c], sb) + uv_s[c]
        ys = _dot_nt(rh_s[c], sb)
        yield
        y = ys + _dot(nrb_s[c], stack4(u)) + yv_s[c]
        yield
        mean = _split_dot_r(y, seg, 2) * (1.0 / RW_HS)
        yield
        d = y - mean
        var = _split_dot_r(d * d, seg, 2) * (1.0 / RW_HS)
        yield
        yn = d * lax.rsqrt(var + A_GN_EPS) * lnw_ref[...] + lnb_ref[...]
        return ((yn + bon_s[c]) * g_s[c]).astype(o_ref.dtype)

    def phase2(i, sbd):
        gens = []
        for j in range(unroll):
            c = i * unroll + j
            sb = sbd.astype(BF16)
            gens.append(chunk_out(c, sb))
            sbd = gc_s[c][0:1, :] * sbd + _dot(sb, gm_s[c]) + hm_s[c]
        for j, o in enumerate(_interleave(gens)):
            r0 = pl.multiple_of((i * unroll + j) * C, C)
            o_ref[pl.ds(r0, C), :] = o
        return sbd

    s_ref[...] = lax.fori_loop(0, n_chunks // unroll, phase2, s0_ref[...])


def rwkv_mix(pa, shift0, s0, layer, mu, w0, w_up, a0, a_up, g_up1, g_up2, k_k, k_a, r_k, ln_w, ln_b, *, batch, seq,
             n_valid, out_rows):
    n_hg = A_WIDTH // RW_L
    nc = seq // RW_C
    col = lambda off: (lambda b, h: (b, off + h))
    fixed = lambda off: (lambda b, h: (b, off))
    pcol = lambda off: (lambda b, h: (layer, 0, off + h))
    pfix = lambda off: (lambda b, h: (layer, 0, off))
    scol = lambda off: (lambda b, h: (b, 0, off + h))
    sfix = lambda off: (lambda b, h: (b, 0, off))
    blk = lambda im: pl.BlockSpec((seq, RW_L), im)
    pblk = lambda rows, im: pl.BlockSpec((None, rows, RW_L), im)
    cs = lambda dt, *shape: pltpu.VMEM((nc,) + shape, dt)
    return pl.pallas_call(
        functools.partial(_rwkv_kernel, n_chunks=nc, n_valid=n_valid, unroll=min(RW_UNROLL, nc)),
        grid=(batch, n_hg),
        in_specs=[blk(col(0)), blk(col(n_hg)), blk(col(2 * n_hg)), blk(fixed(3 * n_hg)), blk(fixed(3 * n_hg + 1)),
                  pblk(1, scol(0)), pblk(1, scol(n_hg)), pblk(1, scol(2 * n_hg)), pblk(1, sfix(3 * n_hg)),
                  pblk(1, sfix(3 * n_hg + 1)),
                  pl.BlockSpec((None, None, RW_L, RW_L), lambda b, h: (b, h, 0, 0)),
                  pblk(1, pcol(0)), pblk(1, pcol(n_hg)), pblk(1, pcol(2 * n_hg)), pblk(1, pfix(3 * n_hg)),
                  pblk(1, pfix(3 * n_hg + 1)),
                  pblk(1, pcol(0)), pblk(RW_L, pcol(0)), pblk(1, pcol(0)), pblk(RW_L, pcol(0)), pblk(RW_L, pcol(0)),
                  pblk(RW_L, pcol(0)), pblk(1, pcol(0)), pblk(1, pcol(0)), pblk(1, pcol(0)), pblk(1, pcol(0)),
                  pblk(1, pcol(0))],
        out_specs=[pl.BlockSpec((seq, RW_L), lambda b, h: (b, h)),
                   pl.BlockSpec((None, None, RW_L, RW_L), lambda b, h: (b, h, 0, 0))],
        out_shape=[jax.ShapeDtypeStruct((out_rows, A_WIDTH), BF16),
                   jax.ShapeDtypeStruct((batch, n_hg, RW_L, RW_L), F32)],
        scratch_shapes=[cs(BF16, RW_C, RW_L), cs(BF16, RW_C, RW_L), cs(BF16, RW_C, 4 * RW_C), cs(BF16, RW_L, RW_L),
                        cs(F32, RW_L, RW_L), cs(F32, RW_C, RW_L), cs(F32, RW_C, RW_L),
                        cs(F32, RW_C, RW_L), cs(F32, RW_C, RW_L), cs(F32, 8, RW_L)],
        compiler_params=_cparams(("parallel", "parallel")),
        name="rwkv_mix",
    )(pa, pa, pa, pa, pa, shift0, shift0, shift0, shift0, shift0, s0, mu, mu, mu, mu, mu,
      w0, w_up, a0, a_up, g_up1, g_up2, k_k, k_a, r_k, ln_w, ln_b)


LRU_ROWS = 64
LRU_L = 256


def _gelu_tanh(x):
    return 0.5 * x * (1.0 + jnp.tanh(np.sqrt(2.0 / np.pi).astype(np.float32) * (x + 0.044715 * (x * x * x))))


def lru_gates(c, wa_ref, ba, wx_ref, bx, lam):
    nb = c.shape[1] // B_BLOCK
    blocks = [c[:, i * B_BLOCK:(i + 1) * B_BLOCK] for i in range(nb)]
    gr = jnp.concatenate([_dot(blocks[i], wa_ref[i]) for i in range(nb)], axis=1) + ba
    gi = jnp.concatenate([_dot(blocks[i], wx_ref[i]) for i in range(nb)], axis=1) + bx
    log_a = -LRU_C * jax.nn.sigmoid(gr) * _softplus(-lam)
    a = jnp.exp(log_a)
    u = jnp.sqrt(-jnp.tanh(log_a) * (a * a + 1.0)) * jax.nn.sigmoid(gi) * c
    return a, u


def _lru_kernel(x_ref, g_ref, tail_ref, h0_ref, cw_ref, cb_ref, wa_ref, ba_ref, wx_ref, bx_ref, lam_ref, o_ref, h_ref, *,
                n_chunks, last_row):
    C, L = LRU_ROWS, LRU_L
    rows = lax.broadcasted_iota(jnp.int32, (C, L), 0)
    cw = cw_ref[...]

    def body(ci, carry):
        tail, h_prev, _ = carry
        r0 = pl.multiple_of(ci * C, C)
        x = x_ref[pl.ds(r0, C), :]
        xe = jnp.concatenate([tail, x], axis=0)
        c = cb_ref[...] + x * cw[3:4, :]
        for s in (1, 2, 3):
            c = c + pltpu.roll(xe, s, axis=0)[8:, :] * cw[3 - s:4 - s, :]
        a, u = lru_gates(c, wa_ref, ba_ref[...], wx_ref, bx_ref[...], lam_ref[...])
        d = 1
        while d < C:
            keep = rows >= d
            a_s = jnp.where(keep, pltpu.roll(a, d, axis=0), 1.0)
            u_s = jnp.where(keep, pltpu.roll(u, d, axis=0), 0.0)
            u = u + a * u_s
            a = a * a_s
            d *= 2
        h = a * h_prev + u
        o_ref[pl.ds(r0, C), :] = (h * _gelu_tanh(g_ref[pl.ds(r0, C), :])).astype(o_ref.dtype)
        return x[C - 8:, :], h[C - 1:C, :], h[last_row:last_row + 1, :]

    _, _, h_last = lax.fori_loop(0, n_chunks, body, (tail_ref[...], h0_ref[...], h0_ref[...]))
    h_ref[...] = h_last


def lru_mix(lru, tail0, h0, layer, conv_w, conv_b, w_a, b_a, w_x, b_x, lam, *, batch, seq, n_valid, out_rows):
    nj = B_WIDTH // LRU_L
    nb = LRU_L // B_BLOCK
    nc = seq // LRU_ROWS
    assert 0 <= n_valid - 1 - (nc - 1) * LRU_ROWS < LRU_ROWS
    vec = pl.BlockSpec((None, 1, LRU_L), lambda b, j: (layer, 0, j))
    wblk = pl.BlockSpec((None, nb, B_BLOCK, B_BLOCK), lambda b, j: (layer, j, 0, 0))
    return pl.pallas_call(
        functools.partial(_lru_kernel, n_chunks=nc, last_row=n_valid - 1 - (nc - 1) * LRU_ROWS),
        grid=(batch, nj),
        in_specs=[pl.BlockSpec((seq, LRU_L), lambda b, j: (b, j)),
                  pl.BlockSpec((seq, LRU_L), lambda b, j: (b, nj + j)),
                  pl.BlockSpec((None, 8, LRU_L), lambda b, j: (b, 0, j)),
                  pl.BlockSpec((None, 1, LRU_L), lambda b, j: (b, 0, j)),
                  pl.BlockSpec((None, 4, LRU_L), lambda b, j: (layer, 0, j)),
                  vec, wblk, vec, wblk, vec, vec],
        out_specs=[pl.BlockSpec((seq, LRU_L), lambda b, j: (b, j)),
                   pl.BlockSpec((None, 1, LRU_L), lambda b, j: (b, 0, j))],
        out_shape=[jax.ShapeDtypeStruct((out_rows, B_WIDTH), BF16),
                   jax.ShapeDtypeStruct((batch, 1, B_WIDTH), F32)],
        compiler_params=_cparams(("parallel", "parallel")),
        name="lru_mix",
    )(lru, lru, tail0, h0, conv_w, conv_b, w_a, b_a, w_x, b_x, lam)


C_BAND = 128
NEG_INF = float("-inf")


def _merge_groups(o_prev, lse_prev, o_new, lse_new):
    m = jnp.maximum(lse_prev, lse_new)
    wp = jnp.exp(lse_prev - m)
    wn = jnp.exp(lse_new - m)
    tot = wp + wn
    return (wp * o_prev + wn * o_new) / tot, m + jnp.log(tot)


def _attn_block(q, kcat, vcat, mask):
    sc = jnp.where(mask, _dot_nt(q, kcat) * C_SCALE, NEG_INF)
    m = jnp.max(sc, axis=-1, keepdims=True)
    p = jnp.exp(sc - m)
    den = jnp.sum(p, axis=-1, keepdims=True)
    o = _dot(p, vcat) / den
    return o, jnp.broadcast_to(m + jnp.log(den), o.shape)


def _attn_prompt_kernel(q0_ref, q1_ref, q2_ref, k0_ref, k1_ref, k2_ref, v0_ref, v1_ref, v2_ref, o_ref, oacc, lacc, *, seq):
    B = C_BAND
    iq = lax.broadcasted_iota(jnp.int32, (B, 2 * B), 0)
    jk = lax.broadcasted_iota(jnp.int32, (B, 2 * B), 1)
    band_ok = (jk >= iq) & (jk <= iq + B)
    iq1 = lax.broadcasted_iota(jnp.int32, (B, B), 0)
    jk1 = lax.broadcasted_iota(jnp.int32, (B, B), 1)
    causal = jk1 <= iq1
    refs = ((q0_ref, k0_ref, v0_ref), (q1_ref, k1_ref, v1_ref), (q2_ref, k2_ref, v2_ref))
    for gi, (_, dil) in enumerate(C_GROUPS):
        q_ref, k_ref, v_ref = refs[gi]
        n_blocks = seq // dil // B
        for r in range(dil):
            for s in range(n_blocks):
                rows = pl.ds(r + dil * B * s, B, stride=dil) if dil > 1 else pl.ds(B * s, B)
                q = q_ref[rows, :]
                if s == 0:
                    o, lse = _attn_block(q, k_ref[rows, :], v_ref[rows, :], causal)
                else:
                    keys = (pl.ds(r + dil * B * (s - 1), 2 * B, stride=dil) if dil > 1 else pl.ds(B * (s - 1), 2 * B))
                    o, lse = _attn_block(q, k_ref[keys, :], v_ref[keys, :], band_ok)
                if gi > 0:
                    o, lse = _merge_groups(oacc[rows, :], lacc[rows, :], o, lse)
                oacc[rows, :] = o
                if gi < C_N_GROUPS - 1:
                    lacc[rows, :] = lse
    o_ref[...] = oacc[...].astype(o_ref.dtype)


def attn_prompt(qn, kn, v, *, batch, seq):
    rows = qn.shape[0]
    spec = lambda gi: pl.BlockSpec((seq, C_HEAD_DIM), lambda b, h: (b, gi * C_HEADS + h))
    g = range(C_N_GROUPS)
    return pl.pallas_call(
        functools.partial(_attn_prompt_kernel, seq=seq),
        grid=(batch, C_HEADS),
        in_specs=[spec(gi) for gi in g] * 3,
        out_specs=pl.BlockSpec((seq, C_HEAD_DIM), lambda b, h: (b, h)),
        out_shape=jax.ShapeDtypeStruct((rows, C_GROUP_WIDTH), BF16),
        scratch_shapes=[pltpu.VMEM((seq, C_HEAD_DIM), F32), pltpu.VMEM((seq, C_HEAD_DIM), F32)],
        compiler_params=_cparams(("parallel", "parallel")),
        name="attn_prompt",
    )(qn, qn, qn, kn, kn, kn, v, v, v)


def _attn_sample_kernel(*refs, lb, dil, n_new, first):
    if first:
        q_ref, kn_ref, vn_ref, kc_ref, vc_ref, o_ref, l_ref = refs
    else:
        q_ref, kn_ref, vn_ref, kc_ref, vc_ref, oin_ref, lin_ref, o_ref, l_ref = refs
    R = q_ref.shape[0]
    qi = lax.broadcasted_iota(jnp.int32, (R, lb), 0)
    ci = lax.broadcasted_iota(jnp.int32, (R, lb), 1)
    off = lb + qi - ci
    cache_ok = (off % dil == 0) & (off <= C_BAND * dil)
    qn_i = lax.broadcasted_iota(jnp.int32, (R, R), 0)
    nn_i = lax.broadcasted_iota(jnp.int32, (R, R), 1)
    new_ok = (nn_i <= qn_i) & ((qn_i - nn_i) % dil == 0) & (nn_i < n_new)
    for h in range(C_HEADS):
        ls = slice(h * C_HEAD_DIM, (h + 1) * C_HEAD_DIM)
        q = q_ref[:, ls]
        sc = jnp.where(cache_ok, _dot_nt(q, kc_ref[:, ls]) * C_SCALE, NEG_INF)
        sn = jnp.where(new_ok, _dot_nt(q, kn_ref[:, ls]) * C_SCALE, NEG_INF)
        m = jnp.maximum(jnp.max(sc, axis=-1, keepdims=True), jnp.max(sn, axis=-1, keepdims=True))
        pc = jnp.exp(sc - m)
        pn = jnp.exp(sn - m)
        den = jnp.sum(pc, axis=-1, keepdims=True) + jnp.sum(pn, axis=-1, keepdims=True)
        o = (_dot(pc, vc_ref[:, ls]) + _dot(pn, vn_ref[:, ls])) / den
        lse = jnp.broadcast_to(m + jnp.log(den), o.shape)
        if not first:
            o, lse = _merge_groups(oin_ref[:, ls], lin_ref[:, ls], o, lse)
        o_ref[:, ls] = o
        l_ref[:, ls] = lse


def attn_sample_group(qn, kn, v, k_cache, v_cache, o_run, lse_run, *, layer, gi, dil, n_new, first):
    R = 8
    batch = qn.shape[0] // R
    lb = k_cache.shape[2]
    W = C_GROUP_WIDTH
    qmap = lambda b: (b, gi)
    omap = lambda b: (b, 0)
    cmap = lambda b: (layer, b, 0, 0)
    in_specs = [pl.BlockSpec((R, W), qmap)] * 3 + [pl.BlockSpec((None, None, lb, W), cmap)] * 2
    args = [qn, kn, v, k_cache, v_cache]
    if not first:
        in_specs += [pl.BlockSpec((R, W), omap)] * 2
        args += [o_run, lse_run]
    return pl.pallas_call(
        functools.partial(_attn_sample_kernel, lb=lb, dil=dil, n_new=n_new, first=first),
        grid=(batch,),
        in_specs=in_specs,
        out_specs=[pl.BlockSpec((R, W), omap)] * 2,
        out_shape=[jax.ShapeDtypeStruct((batch * R, W), F32)] * 2,
        compiler_params=_cparams(("parallel",)),
        name="attn_sample_g%d" % gi,
    )(*args)


def _pad_last(x, n):
    return jnp.pad(x, [(0, 0)] * (x.ndim - 1) + [(0, n - x.shape[-1])])


def _rows_at(w, row0, rows):
    return jnp.pad(w, ((0, 0), (row0, rows - row0 - w.shape[1]), (0, 0)))


def _sample_rows(t, rows_per_batch):
    s = t[N_PROMPT:N_PROMPT + N_SAMPLE].reshape(DEC_BATCH, DEC_SEQ, t.shape[1])
    return jnp.pad(s, ((0, 0), (0, rows_per_batch - DEC_SEQ), (0, 0))).reshape(DEC_BATCH * rows_per_batch, t.shape[1])


def _with_tail(full, sample_out, rows_per_batch):
    w = full.shape[1]
    s = sample_out.reshape(DEC_BATCH, rows_per_batch, w)[:, :DEC_SEQ].reshape(N_SAMPLE, w).astype(full.dtype)
    tail = jnp.pad(s, ((0, M_PAD - N_PROMPT - N_SAMPLE), (0, 0)))
    return lax.dynamic_update_slice(full, tail, (N_PROMPT, 0))


def _state_to_blockdiag(s):
    b = s.shape[0]
    s5 = s.reshape(b, A_HEADS // 4, 4, A_HEAD_SIZE, A_HEAD_SIZE)
    return jnp.einsum('bgivk,ij->bgivjk', s5, jnp.eye(4, dtype=s.dtype)).reshape(b, A_HEADS // 4, RW_L, RW_L)


def _blockdiag_to_state(s):
    b = s.shape[0]
    s6 = s.reshape(b, A_HEADS // 4, 4, A_HEAD_SIZE, 4, A_HEAD_SIZE)
    return jnp.einsum('bgivik->bgivk', s6).reshape(b, A_HEADS, A_HEAD_SIZE, A_HEAD_SIZE)


def kernel(x_prompt, x_sample, state_rwkv_shift, state_rwkv_wkv, state_lru_conv, state_lru_h, cache_dil1_k, cache_dil1_v, cache_dil2_k, cache_dil2_v, cache_dil3_k, cache_dil3_v, norm_ffn1, ffn1_w_gate, ffn1_w_up, ffn1_w_down, norm_mix, w_in, rwkv_mu, rwkv_w0, rwkv_w_up, rwkv_a0, rwkv_a_up, rwkv_g_up, rwkv_k_k, rwkv_k_a, rwkv_r_k, rwkv_ln_w, rwkv_ln_b, lru_conv_w, lru_conv_b, lru_w_a, lru_b_a, lru_w_x, lru_b_x, lru_lambda, attn_q_norm, attn_k_norm, w_br_a, w_br_b, w_br_c, w_out, norm_ffn2, ffn2_w_gate, ffn2_w_up, ffn2_w_down):
    L = DEPTH
    caches = ((cache_dil1_k, cache_dil1_v), (cache_dil2_k, cache_dil2_v), (cache_dil3_k, cache_dil3_v))

    wd1, wd2 = ffn1_w_down.astype(BF16), ffn2_w_down.astype(BF16)
    w_rest = w_in[:, :, A_SHIFT_WIDTH:].astype(BF16)
    o = tuple(c - A_SHIFT_WIDTH for c in IN_OFFSETS)
    wba, wbb, wbc, wo = (w.astype(BF16) for w in (w_br_a, w_br_b, w_br_c, w_out))

    vec = lambda t: t.reshape(L, 1, -1)
    g1, gm, g2 = vec(norm_ffn1), vec(norm_mix), vec(norm_ffn2)
    zw0 = A_SPLIT_OFFSETS[2]
    za0, zg0 = A_SPLIT_OFFSETS[3] - zw0, A_SPLIT_OFFSETS[4] - zw0
    g_split = RW_L - zg0
    rw_params = (vec(_pad_last(rwkv_mu, PA_W)), vec(rwkv_w0), _rows_at(rwkv_w_up, 0, RW_L), vec(rwkv_a0),
                 _rows_at(rwkv_a_up, za0, RW_L), _rows_at(rwkv_g_up[:, :g_split], zg0, RW_L),
                 _rows_at(rwkv_g_up[:, g_split:], 0, RW_L), vec(rwkv_k_k), vec(rwkv_k_a), vec(rwkv_r_k),
                 vec(rwkv_ln_w), vec(rwkv_ln_b))
    lru_params = (lru_conv_w, vec(lru_conv_b), lru_w_a, vec(lru_b_a), lru_w_x, vec(lru_b_x), vec(lru_lambda))
    q_gain = attn_q_norm.reshape(L, C_N_GROUPS, 1, C_HEAD_DIM)
    k_gain = attn_k_norm.reshape(L, C_N_GROUPS, 1, C_HEAD_DIM)

    shift_p0 = jnp.zeros((BATCH, 1, PA_W), F32)
    wkv_p0 = jnp.zeros((BATCH, A_WIDTH // RW_L, RW_L, RW_L), F32)
    tail_p0 = jnp.zeros((BATCH, 8, B_WIDTH), F32)
    h_p0 = jnp.zeros((BATCH, 1, B_WIDTH), F32)

    x = jnp.concatenate([x_prompt.reshape(N_PROMPT, D_MODEL), x_sample.reshape(N_SAMPLE, D_MODEL),
                         jnp.zeros((M_PAD - N_PROMPT - N_SAMPLE, D_MODEL), F32)], axis=0)

    outs_p, outs_s = [], []
    for l in range(L):
        u = rmsnorm_rows(x, g1, l)
        hff = ffn_gateup(u, ffn1_w_gate, ffn1_w_up, l)
        h = matmul_resid(hff, wd1, x, l, 512, D_FF // 2, 0.5, "ffn1_down")
        u = rmsnorm_rows(h, gm, l)
        pa = matmul_fullk(u, w_in, l, 512, F32, "proj_rwkv", n=PA_W)
        lru = matmul_fullk(u, w_rest, l, 1024, F32, "proj_lru", col0=o[0], n=2 * B_WIDTH)
        qn = matmul_qknorm(u, w_rest, q_gain, l, "proj_q", col0=o[2])
        kn = matmul_qknorm(u, w_rest, k_gain, l, "proj_k", col0=o[3])
        vv = matmul_fullk(u, w_rest, l, 1024, F32, "proj_v", col0=o[4], n=C_QKV_WIDTH)
        gates = matmul_fullk(u, w_rest, l, 1024, F32, "proj_gates", col0=o[5], n=3 * D_MODEL)

        oa, wkv_p = rwkv_mix(pa, shift_p0, wkv_p0, l, *rw_params, batch=BATCH, seq=SEQ, n_valid=SEQ, out_rows=M_PAD)
        oa_s, wkv_s = rwkv_mix(_sample_rows(pa, RW_C), _pad_last(state_rwkv_shift[l], PA_W)[:, None, :],
                               _state_to_blockdiag(state_rwkv_wkv[l]), l, *rw_params, batch=DEC_BATCH, seq=RW_C,
                               n_valid=DEC_SEQ, out_rows=DEC_BATCH * RW_C)
        oa = _with_tail(oa, oa_s, RW_C)
        shift_p = pa[SEQ - 1:N_PROMPT:SEQ]
        shift_s = pa[N_PROMPT + DEC_SEQ - 1:N_PROMPT + N_SAMPLE:DEC_SEQ]

        ob, lh_p = lru_mix(lru, tail_p0, h_p0, l, *lru_params, batch=BATCH, seq=SEQ, n_valid=SEQ, out_rows=M_PAD)
        tail_s = jnp.pad(state_lru_conv[l], ((0, 0), (8 - (CONV_WIDTH - 1), 0), (0, 0)))
        ob_s, lh_s = lru_mix(_sample_rows(lru, LRU_ROWS), tail_s, state_lru_h[l][:, None, :], l, *lru_params,
                             batch=DEC_BATCH, seq=LRU_ROWS, n_valid=DEC_SEQ, out_rows=DEC_BATCH * LRU_ROWS)
        ob = _with_tail(ob, ob_s, LRU_ROWS)
        lx_s = lru[N_PROMPT:N_PROMPT + N_SAMPLE, :B_WIDTH].reshape(DEC_BATCH, DEC_SEQ, B_WIDTH)
        conv_p = jnp.stack([lru[(b + 1) * SEQ - (CONV_WIDTH - 1):(b + 1) * SEQ, :B_WIDTH] for b in range(BATCH)])
        conv_s = jnp.concatenate([state_lru_conv[l], lx_s], axis=1)[:, -(CONV_WIDTH - 1):]

        oc = attn_prompt(qn, kn, vv, batch=BATCH, seq=SEQ)
        qs, ks, vs = _sample_rows(qn, 8), _sample_rows(kn, 8), _sample_rows(vv, 8)
        o_run = lse_run = None
        for gi, (window, dil) in enumerate(C_GROUPS):
            kc, vc = (c.reshape(L, DEC_BATCH, window, C_GROUP_WIDTH) for c in caches[gi])
            o_run, lse_run = attn_sample_group(qs, ks, vs, kc, vc, o_run, lse_run, layer=l, gi=gi, dil=dil,
                                               n_new=DEC_SEQ, first=gi == 0)
        oc = _with_tail(oc, o_run, 8)

        new_p = [shift_p[:, :A_SHIFT_WIDTH], _blockdiag_to_state(wkv_p), conv_p, lh_p[:, 0]]
        new_s = [shift_s[:, :A_SHIFT_WIDTH], _blockdiag_to_state(wkv_s), conv_s, lh_s[:, 0]]
        for gi, (window, _) in enumerate(C_GROUPS):
            keep = min(window, SEQ)
            cols = slice(gi * C_GROUP_WIDTH, (gi + 1) * C_GROUP_WIDTH)
            for t, cache in ((kn, caches[gi][0]), (vv, caches[gi][1])):
                kept = jnp.stack([t[(b + 1) * SEQ - keep:(b + 1) * SEQ, cols] for b in range(BATCH)])
                new_p.append(kept.reshape(BATCH, keep, C_HEADS, C_HEAD_DIM))
                fresh = t[N_PROMPT:N_PROMPT + N_SAMPLE, cols].reshape(DEC_BATCH, DEC_SEQ, C_HEADS, C_HEAD_DIM)
                new_s.append(jnp.concatenate([cache[l][:, DEC_SEQ:], fresh], axis=1))
        outs_p.append(new_p)
        outs_s.append(new_s)

        merged = branch_merge(oa, ob, oc, wba, wbb, wbc, gates, l)
        h = matmul_resid(merged, wo, h, l, 512, D_MODEL, 1.0, "w_out")
        u = rmsnorm_rows(h, g2, l)
        hff = ffn_gateup(u, ffn2_w_gate, ffn2_w_up, l)
        x = matmul_resid(hff, wd2, h, l, 512, D_FF // 2, 0.5, "ffn2_down")

    res = [x[:N_PROMPT].reshape(BATCH, SEQ, D_MODEL),
           x[N_PROMPT:N_PROMPT + N_SAMPLE].reshape(DEC_BATCH, DEC_SEQ, D_MODEL)]
    for i in range(10):
        res.append(jnp.stack([o_[i] for o_ in outs_p]))
        res.append(jnp.stack([o_[i] for o_ in outs_s]))
    return tuple(res)
```

```python
import functools

import numpy as np
import jax
import jax.numpy as jnp
from jax import lax
from jax.experimental import pallas as pl
from jax.experimental.pallas import tpu as pltpu

F32 = jnp.float32
BF16 = jnp.bfloat16

D_MODEL = 4096
BATCH = 4
SEQ = 2048
DEPTH = 4
DEC_BATCH = 8
DEC_SEQ = 4
PAST_LEN = 8192

A_HEAD_SIZE = 64
A_WIDTH = D_MODEL // 2
A_HEADS = A_WIDTH // A_HEAD_SIZE
A_DECAY_RANK = 96
A_ICLR_RANK = 96
A_GATE_RANK = 256
A_GN_EPS = 64e-5
A_SHIFT_WIDTH = 3 * A_WIDTH + A_DECAY_RANK + A_ICLR_RANK + A_GATE_RANK
A_SPLIT_OFFSETS = (A_WIDTH, 2 * A_WIDTH, 3 * A_WIDTH, 3 * A_WIDTH + A_DECAY_RANK,
                   3 * A_WIDTH + A_DECAY_RANK + A_ICLR_RANK)
B_WIDTH = D_MODEL // 2
B_BLOCKS = 16
B_BLOCK = B_WIDTH // B_BLOCKS
CONV_WIDTH = 4
LRU_C = 8.0
C_HEADS = 8
C_HEAD_DIM = 128
C_GROUPS = ((128, 1), (512, 4), (2048, 16))
C_N_GROUPS = 3
C_GROUP_WIDTH = C_HEADS * C_HEAD_DIM
C_QKV_WIDTH = C_N_GROUPS * C_GROUP_WIDTH
C_SCALE = C_HEAD_DIM ** -0.5
QK_EPS = 1e-6
D_FF = 11008
NORM_EPS = 1e-6
IN_SPLITS = (A_SHIFT_WIDTH, B_WIDTH, B_WIDTH, C_QKV_WIDTH, C_QKV_WIDTH, C_QKV_WIDTH, D_MODEL, D_MODEL, D_MODEL)
IN_OFFSETS = tuple(int(s) for s in np.cumsum(IN_SPLITS)[:-1])

LANE = 128
VMEM_LIMIT_BYTES = 56 * 1024 * 1024

N_PROMPT = BATCH * SEQ
N_SAMPLE = DEC_BATCH * DEC_SEQ
ROW_TILE = 1040
M_PAD = 8 * ROW_TILE
A_SHIFT_PAD = 13 * 512

assert M_PAD >= N_PROMPT + N_SAMPLE


def _cparams(sem):
    return pltpu.CompilerParams(dimension_semantics=sem, vmem_limit_bytes=VMEM_LIMIT_BYTES)


def _rmsnorm_kernel(x_ref, g_ref, o_ref):
    x = x_ref[...]
    ms = jnp.mean(x * x, axis=-1, keepdims=True)
    o_ref[...] = (x * lax.rsqrt(ms + NORM_EPS) * g_ref[...]).astype(o_ref.dtype)


def rmsnorm_rows(x, g, layer):
    m, d = x.shape
    tm = 208
    return pl.pallas_call(
        _rmsnorm_kernel,
        grid=(m // tm,),
        in_specs=[pl.BlockSpec((tm, d), lambda i: (i, 0)),
                  pl.BlockSpec((None, 1, d), lambda i: (layer, 0, 0))],
        out_specs=pl.BlockSpec((tm, d), lambda i: (i, 0)),
        out_shape=jax.ShapeDtypeStruct((m, d), BF16),
        compiler_params=_cparams(("parallel",)),
        name="rmsnorm",
    )(x, g)


def _mm_kernel(x_ref, w_ref, o_ref):
    o_ref[...] = jnp.dot(x_ref[...], w_ref[...].astype(BF16), preferred_element_type=F32).astype(o_ref.dtype)


def matmul_fullk(x, w, layer, tn, out_dtype, name, col0=0, n=None):
    m, k = x.shape
    n = w.shape[2] if n is None else n
    c0 = col0 // tn
    assert c0 * tn == col0 and n % tn == 0
    return pl.pallas_call(
        _mm_kernel,
        grid=(m // ROW_TILE, n // tn),
        in_specs=[pl.BlockSpec((ROW_TILE, k), lambda i, j: (i, 0)),
                  pl.BlockSpec((None, k, tn), lambda i, j: (layer, 0, c0 + j))],
        out_specs=pl.BlockSpec((ROW_TILE, tn), lambda i, j: (i, j)),
        out_shape=jax.ShapeDtypeStruct((m, n), out_dtype),
        compiler_params=_cparams(("parallel", "parallel")),
        name=name,
    )(x, w)


def _qknorm_mm_kernel(x_ref, w_ref, g_ref, o_ref):
    acc = jnp.dot(x_ref[...], w_ref[...], preferred_element_type=F32)
    gain = g_ref[...]
    for h in range(acc.shape[1] // C_HEAD_DIM):
        ls = slice(h * C_HEAD_DIM, (h + 1) * C_HEAD_DIM)
        t = acc[:, ls]
        o_ref[:, ls] = t * lax.rsqrt(jnp.mean(t * t, axis=-1, keepdims=True) + QK_EPS) * gain


def matmul_qknorm(x, w, gain, layer, name, col0):
    m, k = x.shape
    n = C_QKV_WIDTH
    tn = C_GROUP_WIDTH
    c0 = col0 // tn
    assert c0 * tn == col0
    return pl.pallas_call(
        _qknorm_mm_kernel,
        grid=(m // ROW_TILE, n // tn),
        in_specs=[pl.BlockSpec((ROW_TILE, k), lambda i, j: (i, 0)),
                  pl.BlockSpec((None, k, tn), lambda i, j: (layer, 0, c0 + j)),
                  pl.BlockSpec((None, None, 1, C_HEAD_DIM), lambda i, j: (layer, j, 0, 0))],
        out_specs=pl.BlockSpec((ROW_TILE, tn), lambda i, j: (i, j)),
        out_shape=jax.ShapeDtypeStruct((m, n), F32),
        compiler_params=_cparams(("parallel", "parallel")),
        name=name,
    )(x, w, gain)


W_REST_TILE = 3 * LANE
W_REST0 = (A_SHIFT_WIDTH // W_REST_TILE) * W_REST_TILE
W_REST_SHIFT = A_SHIFT_WIDTH - W_REST0
W_REST_COLS = pl.cdiv(sum(IN_SPLITS) - A_SHIFT_WIDTH, W_REST_TILE) * W_REST_TILE

assert 0 < W_REST_SHIFT < LANE and W_REST0 % W_REST_TILE == 0


def _realign_kernel(a_ref, b_ref, o_ref):
    a = a_ref[...]
    b = b_ref[...]
    o_ref[...] = jnp.concatenate([a[:, W_REST_SHIFT:], b[:, :W_REST_SHIFT]], axis=1).astype(o_ref.dtype)


def realign_cast(w):
    nl, k, n = w.shape
    nj = W_REST_COLS // W_REST_TILE
    c0 = W_REST0 // W_REST_TILE
    last = (n - 1) // LANE
    return pl.pallas_call(
        _realign_kernel,
        grid=(nl, nj),
        in_specs=[pl.BlockSpec((None, k, W_REST_TILE), lambda l, j: (l, 0, c0 + j)),
                  pl.BlockSpec((None, k, LANE), lambda l, j: (l, 0, jnp.minimum((c0 + j + 1) * 3, last)))],
        out_specs=pl.BlockSpec((None, k, W_REST_TILE), lambda l, j: (l, 0, j)),
        out_shape=jax.ShapeDtypeStruct((nl, k, W_REST_COLS), BF16),
        compiler_params=_cparams(("parallel", "parallel")),
        name="realign_cast",
    )(w, w)


def _gateup_kernel(x_ref, wg_ref, wu_ref, o_ref):
    x = x_ref[...]
    g = jnp.dot(x, wg_ref[...].astype(BF16), preferred_element_type=F32)
    u = jnp.dot(x, wu_ref[...].astype(BF16), preferred_element_type=F32)
    o_ref[...] = (g * jax.nn.sigmoid(g) * u).astype(o_ref.dtype)


def ffn_gateup(x, wg, wu, layer):
    m, k = x.shape
    n = wg.shape[2]
    tn = 512
    return pl.pallas_call(
        _gateup_kernel,
        grid=(m // ROW_TILE, pl.cdiv(n, tn)),
        in_specs=[pl.BlockSpec((ROW_TILE, k), lambda i, j: (i, 0), pipeline_mode=pl.Buffered(1)),
                  pl.BlockSpec((None, k, tn), lambda i, j: (layer, 0, j)),
                  pl.BlockSpec((None, k, tn), lambda i, j: (layer, 0, j))],
        out_specs=pl.BlockSpec((ROW_TILE, tn), lambda i, j: (i, j)),
        out_shape=jax.ShapeDtypeStruct((m, n), BF16),
        compiler_params=_cparams(("parallel", "parallel")),
        name="ffn_gateup",
    )(x, wg, wu)


def _mm_resid_kernel(x_ref, w_ref, r_ref, o_ref, *, scale, k_total):
    kk = pl.program_id(2)
    tk = x_ref.shape[1]
    n_k = pl.cdiv(k_total, tk)

    def accumulate(x, w):
        p = scale * jnp.dot(x, w, preferred_element_type=F32)

        @pl.when(kk == 0)
        def _():
            o_ref[...] = r_ref[...] + p

        @pl.when(kk > 0)
        def _():
            o_ref[...] += p

    if k_total % tk == 0:
        accumulate(x_ref[...], w_ref[...])
    else:
        @pl.when(kk < n_k - 1)
        def _():
            accumulate(x_ref[...], w_ref[...])

        @pl.when(kk == n_k - 1)
        def _():
            valid = k_total - (n_k - 1) * tk
            x, w = x_ref[...], w_ref[...]
            x = jnp.where(lax.broadcasted_iota(jnp.int32, x.shape, 1) < valid, x, jnp.zeros((), x.dtype))
            w = jnp.where(lax.broadcasted_iota(jnp.int32, w.shape, 0) < valid, w, jnp.zeros((), w.dtype))
            accumulate(x, w)


def matmul_resid(x, w, resid, layer, tn, tk, scale, name):
    m, k = x.shape
    n = w.shape[2]
    return pl.pallas_call(
        functools.partial(_mm_resid_kernel, scale=scale, k_total=k),
        grid=(m // ROW_TILE, n // tn, pl.cdiv(k, tk)),
        in_specs=[pl.BlockSpec((ROW_TILE, tk), lambda i, j, kk: (i, kk)),
                  pl.BlockSpec((None, tk, tn), lambda i, j, kk: (layer, kk, j)),
                  pl.BlockSpec((ROW_TILE, tn), lambda i, j, kk: (i, j))],
        out_specs=pl.BlockSpec((ROW_TILE, tn), lambda i, j, kk: (i, j)),
        out_shape=jax.ShapeDtypeStruct((m, n), F32),
        compiler_params=_cparams(("parallel", "parallel", "arbitrary")),
        name=name,
    )(x, w, resid)


def _merge_kernel(oa_ref, ob_ref, oc_ref, wa_ref, wb_ref, wc_ref, ga_ref, gb_ref, gc_ref, o_ref):
    a = jnp.dot(oa_ref[...], wa_ref[...], preferred_element_type=F32)
    b = jnp.dot(ob_ref[...], wb_ref[...], preferred_element_type=F32)
    c = jnp.dot(oc_ref[...], wc_ref[...], preferred_element_type=F32)
    o_ref[...] = (jax.nn.sigmoid(ga_ref[...]) * a + jax.nn.sigmoid(gb_ref[...]) * b
                  + jax.nn.sigmoid(gc_ref[...]) * c).astype(o_ref.dtype)


def branch_merge(oa, ob, oc, wa, wb, wc, gates, layer):
    m = oa.shape[0]
    n = wa.shape[2]
    tn = 512
    nb = n // tn
    row = lambda i, j: (i, 0)
    wcol = lambda i, j: (layer, 0, j)
    return pl.pallas_call(
        _merge_kernel,
        grid=(m // ROW_TILE, nb),
        in_specs=[pl.BlockSpec((ROW_TILE, oa.shape[1]), row),
                  pl.BlockSpec((ROW_TILE, ob.shape[1]), row),
                  pl.BlockSpec((ROW_TILE, oc.shape[1]), row),
                  pl.BlockSpec((None, wa.shape[1], tn), wcol),
                  pl.BlockSpec((None, wb.shape[1], tn), wcol),
                  pl.BlockSpec((None, wc.shape[1], tn), wcol),
                  pl.BlockSpec((ROW_TILE, tn), lambda i, j: (i, j)),
                  pl.BlockSpec((ROW_TILE, tn), lambda i, j: (i, nb + j)),
                  pl.BlockSpec((ROW_TILE, tn), lambda i, j: (i, 2 * nb + j))],
        out_specs=pl.BlockSpec((ROW_TILE, tn), lambda i, j: (i, j)),
        out_shape=jax.ShapeDtypeStruct((m, n), BF16),
        compiler_params=_cparams(("parallel", "parallel")),
        name="branch_merge",
    )(oa, ob, oc, wa, wb, wc, gates, gates, gates)


RW_C = 64
RW_L = 256
RW_HS = A_HEAD_SIZE
RW_UNROLL = 8
PA_W = A_SHIFT_PAD


def _dot(a, b):
    return jnp.dot(a.astype(BF16), b.astype(BF16), preferred_element_type=F32)


def _dot_nt(a, b):
    return lax.dot_general(a.astype(BF16), b.astype(BF16), (((1,), (1,)), ((), ())), preferred_element_type=F32)


def _dot_tn(a, b):
    return lax.dot_general(a.astype(BF16), b.astype(BF16), (((0,), (0,)), ((), ())), preferred_element_type=F32)


def _split_dot(e, x, parts):
    acc = None
    rem = x
    for _ in range(parts):
        p = rem.astype(BF16)
        rem = rem - p.astype(F32)
        d = jnp.dot(e, p, preferred_element_type=F32)
        acc = d if acc is None else acc + d
    return acc


def _split_dot_r(x, e, parts):
    acc = None
    rem = x
    for _ in range(parts):
        p = rem.astype(BF16)
        rem = rem - p.astype(F32)
        d = jnp.dot(p, e, preferred_element_type=F32)
        acc = d if acc is None else acc + d
    return acc


def _softplus(x):
    return jnp.maximum(x, 0.0) + jnp.log1p(jnp.exp(-jnp.abs(x)))


def _interleave(gens):
    results = [None] * len(gens)
    live = list(range(len(gens)))
    while live:
        for i in list(live):
            try:
                next(gens[i])
            except StopIteration as done:
                results[i] = done.value
                live.remove(i)
    return results


def rwkv_prep(r, k, v, z1, z2, w0, w_up, a0, a_up, g_up1, g_up2, k_k, k_a, r_k, seg):
    logw = -_softplus(-(w0 + _dot(jnp.tanh(z1), w_up))) - 0.5
    ld = -jnp.exp(logw)
    a = jax.nn.sigmoid(a0 + _dot(z1, a_up))
    g = _dot(jax.nn.sigmoid(z1), g_up1) + _dot(jax.nn.sigmoid(z2), g_up2)
    kk = k * k_k
    ss = _split_dot_r(kk * kk, seg, 2)
    yield
    kk = kk / jnp.maximum(jnp.sqrt(ss), 1e-12)
    k_h = k * (1.0 + (a - 1.0) * k_a)
    bonus = _split_dot_r(r * k_h * r_k, seg, 2) * v
    yield
    return ld, -kk, kk * a, k_h, g, bonus


def _rwkv_kernel(r_ref, k_ref, v_ref, wa_ref, zg_ref, sh_r, sh_k, sh_v, sh_wa, sh_g, s0_ref,
                 mu_r, mu_k, mu_v, mu_wa, mu_g,
                 w0_ref, wup_ref, a0_ref, aup_ref, gup1_ref, gup2_ref, kk_ref, ka_ref, rk_ref, lnw_ref, lnb_ref,
                 o_ref, s_ref,
                 at_s, rh_s, nrb_s, gm_s, hm_s, uv_s, yv_s, bon_s, g_s, gc_s, *, n_chunks, n_valid, unroll, batch):
    @pl.when(pl.program_id(0) == batch)
    def _():
        o_ref[...] = jnp.zeros(o_ref.shape, o_ref.dtype)
        s_ref[...] = jnp.zeros(s_ref.shape, s_ref.dtype)

    @pl.when(pl.program_id(0) < batch)
    def _():
        _rwkv_body(r_ref, k_ref, v_ref, wa_ref, zg_ref, sh_r, sh_k, sh_v, sh_wa, sh_g, s0_ref, mu_r, mu_k, mu_v, mu_wa, mu_g,
                   w0_ref, wup_ref, a0_ref, aup_ref, gup1_ref, gup2_ref, kk_ref, ka_ref, rk_ref, lnw_ref, lnb_ref, o_ref, s_ref,
                   at_s, rh_s, nrb_s, gm_s, hm_s, uv_s, yv_s, bon_s, g_s, gc_s, n_chunks=n_chunks, n_valid=n_valid,
                   unroll=unroll)


def _rwkv_body(r_ref, k_ref, v_ref, wa_ref, zg_ref, sh_r, sh_k, sh_v, sh_wa, sh_g, s0_ref, mu_r, mu_k, mu_v, mu_wa, mu_g,
               w0_ref, wup_ref, a0_ref, aup_ref, gup1_ref, gup2_ref, kk_ref, ka_ref, rk_ref, lnw_ref, lnb_ref, o_ref, s_ref,
               at_s, rh_s, nrb_s, gm_s, hm_s, uv_s, yv_s, bon_s, g_s, gc_s, *, n_chunks, n_valid, unroll):
    C, L = RW_C, RW_L
    row = lax.broadcasted_iota(jnp.int32, (C, 4 * C), 0)
    col = lax.broadcasted_iota(jnp.int32, (C, 4 * C), 1) % C
    strict = col < row
    incl = col <= row
    eye_cat = (col == row).astype(F32)
    bd_mask = (lax.broadcasted_iota(jnp.int32, (L, L), 0) // RW_HS) == (lax.broadcasted_iota(jnp.int32, (L, L), 1) // RW_HS)
    seg = bd_mask.astype(BF16)
    ltri = (lax.broadcasted_iota(jnp.int32, (C, C), 1) <= lax.broadcasted_iota(jnp.int32, (C, C), 0)).astype(BF16)
    first_row = lax.broadcasted_iota(jnp.int32, (C, L), 0) == 0

    def stack4(z):
        zb = z.astype(BF16)
        return jnp.where(bd_mask, jnp.concatenate([zb, zb, zb, zb], axis=0), jnp.zeros((), BF16))

    def shifted(x_ref, mu_ref, r0, last):
        x = x_ref[pl.ds(r0, C), :]
        prev = jnp.where(first_row, last, pltpu.roll(x, 1, axis=0))
        return x + (prev - x) * mu_ref[...], x[C - 1:C, :]

    def chunk_terms(xr, xk, xv, xwa, xg):
        ld, a_vec, b_vec, k_h, g, bonus = yield from rwkv_prep(
            xr, xk, xv, xwa, xg, w0_ref[...], wup_ref[...], a0_ref[...], aup_ref[...],
            gup1_ref[...], gup2_ref[...], kk_ref[...], ka_ref[...], rk_ref[...], seg)
        if n_valid < n_chunks * C:
            live = lax.broadcasted_iota(jnp.int32, (C, L), 0) < n_valid
            ld, a_vec, b_vec, k_h = (jnp.where(live, t_, 0.0) for t_ in (ld, a_vec, b_vec, k_h))
        cum = _split_dot(ltri, ld, 3)
        yield
        cum_c = cum[C - 1:C, :]
        ah = a_vec * jnp.exp(cum - ld)
        rh = xr * jnp.exp(cum)
        inv = jnp.exp(-cum)
        rel = jnp.exp(cum_c - cum)
        ar = jnp.concatenate([ah, rh], axis=0)
        pb = _dot_nt(ar, stack4(b_vec * inv))
        pk = _dot_nt(ar, stack4(k_h * inv))
        yield
        nab = jnp.where(strict, pb[:C], 0.0)
        nrb = jnp.where(incl, pb[C:], 0.0)
        nak = jnp.where(strict, pk[:C], 0.0)
        nrk = jnp.where(incl, pk[C:], 0.0)
        t = eye_cat + nab
        n = _dot(nab, stack4(nab))
        nakv = _dot(nak, stack4(xv))
        yv = _dot(nrk, stack4(xv))
        yield
        for _ in range(5):
            p = _dot(jnp.concatenate([n, t], axis=0), stack4(n))
            yield
            t = t + p[C:]
            n = p[:C]
        at = _dot(t, stack4(ah))
        uv = _dot(t, stack4(nakv))
        yield
        bt = (b_vec * rel).astype(BF16)
        kt = (k_h * rel).astype(BF16)
        gm = _dot_tn(at, bt)
        hm = _dot_tn(jnp.concatenate([uv.astype(BF16), xv.astype(BF16)], axis=0), jnp.concatenate([bt, kt], axis=0))
        yield
        return (at.astype(BF16), rh.astype(BF16), nrb.astype(BF16), jnp.where(bd_mask, gm, 0.0).astype(BF16),
                jnp.where(bd_mask, hm, 0.0), uv, yv, bonus, g, jnp.broadcast_to(jnp.exp(cum_c), (8, L)))

    scratch = (at_s, rh_s, nrb_s, gm_s, hm_s, uv_s, yv_s, bon_s, g_s, gc_s)
    in_refs = ((r_ref, mu_r), (k_ref, mu_k), (v_ref, mu_v), (wa_ref, mu_wa), (zg_ref, mu_g))

    def phase1(i, carry):
        gens = []
        for j in range(unroll):
            r0 = pl.multiple_of((i * unroll + j) * C, C)
            xs = []
            nxt = []
            for (x_ref, mu_ref), last in zip(in_refs, carry):
                xm, tail = shifted(x_ref, mu_ref, r0, last)
                xs.append(xm)
                nxt.append(tail)
            carry = tuple(nxt)
            gens.append(chunk_terms(*xs))
        for j, vals in enumerate(_interleave(gens)):
            for ref, val in zip(scratch, vals):
                ref[i * unroll + j] = val
        return carry

    lax.fori_loop(0, n_chunks // unroll, phase1, (sh_r[...], sh_k[...], sh_v[...], sh_wa[...], sh_g[...]))

    def chunk_out(c, sb):
        u = _dot_nt(at_s[c], sb) + uv_s[c]
        ys = _dot_nt(rh_s[c], sb)
        yield
        y = ys + _dot(nrb_s[c], stack4(u)) + yv_s[c]
        yield
        mean = _split_dot_r(y, seg, 2) * (1.0 / RW_HS)
        yield
        d = y - mean
        var = _split_dot_r(d * d, seg, 2) * (1.0 / RW_HS)
        yield
        yn = d * lax.rsqrt(var + A_GN_EPS) * lnw_ref[...] + lnb_ref[...]
        return ((yn + bon_s[c]) * g_s[c]).astype(o_ref.dtype)

    def phase2(i, sbd):
        gens = []
        for j in range(unroll):
            c = i * unroll + j
            sb = sbd.astype(BF16)
            gens.append(chunk_out(c, sb))
            sbd = gc_s[c][0:1, :] * sbd + _dot(sb, gm_s[c]) + hm_s[c]
        for j, o in enumerate(_interleave(gens)):
            r0 = pl.multiple_of((i * unroll + j) * C, C)
            o_ref[pl.ds(r0, C), :] = o
        return sbd

    s_ref[...] = lax.fori_loop(0, n_chunks // unroll, phase2, s0_ref[...])


def rwkv_mix(pa, shift0, s0, layer, mu, w0, w_up, a0, a_up, g_up1, g_up2, k_k, k_a, r_k, ln_w, ln_b, *, batch, seq,
             n_valid, out_rows):
    n_hg = A_WIDTH // RW_L
    nc = seq // RW_C
    nb = batch + (out_rows > batch * seq)
    cb = lambda b: jnp.minimum(b, batch - 1)
    col = lambda off: (lambda b, h: (cb(b), off + h))
    fixed = lambda off: (lambda b, h: (cb(b), off))
    pcol = lambda off: (lambda b, h: (layer, 0, off + h))
    pfix = lambda off: (lambda b, h: (layer, 0, off))
    scol = lambda off: (lambda b, h: (cb(b), 0, off + h))
    sfix = lambda off: (lambda b, h: (cb(b), 0, off))
    blk = lambda im: pl.BlockSpec((seq, RW_L), im)
    pblk = lambda rows, im: pl.BlockSpec((None, rows, RW_L), im)
    cs = lambda dt, *shape: pltpu.VMEM((nc,) + shape, dt)
    return pl.pallas_call(
        functools.partial(_rwkv_kernel, n_chunks=nc, n_valid=n_valid, unroll=min(RW_UNROLL, nc), batch=batch),
        grid=(nb, n_hg),
        in_specs=[blk(col(0)), blk(col(n_hg)), blk(col(2 * n_hg)), blk(fixed(3 * n_hg)), blk(fixed(3 * n_hg + 1)),
                  pblk(1, scol(0)), pblk(1, scol(n_hg)), pblk(1, scol(2 * n_hg)), pblk(1, sfix(3 * n_hg)),
                  pblk(1, sfix(3 * n_hg + 1)),
                  pl.BlockSpec((None, None, RW_L, RW_L), lambda b, h: (cb(b), h, 0, 0)),
                  pblk(1, pcol(0)), pblk(1, pcol(n_hg)), pblk(1, pcol(2 * n_hg)), pblk(1, pfix(3 * n_hg)),
                  pblk(1, pfix(3 * n_hg + 1)),
                  pblk(1, pcol(0)), pblk(RW_L, pcol(0)), pblk(1, pcol(0)), pblk(RW_L, pcol(0)), pblk(RW_L, pcol(0)),
                  pblk(RW_L, pcol(0)), pblk(1, pcol(0)), pblk(1, pcol(0)), pblk(1, pcol(0)), pblk(1, pcol(0)),
                  pblk(1, pcol(0))],
        out_specs=[pl.BlockSpec((seq, RW_L), lambda b, h: (b, h)),
                   pl.BlockSpec((None, None, RW_L, RW_L), lambda b, h: (b, h, 0, 0))],
        out_shape=[jax.ShapeDtypeStruct((out_rows, A_WIDTH), BF16),
                   jax.ShapeDtypeStruct((nb, n_hg, RW_L, RW_L), F32)],
        scratch_shapes=[cs(BF16, RW_C, RW_L), cs(BF16, RW_C, RW_L), cs(BF16, RW_C, 4 * RW_C), cs(BF16, RW_L, RW_L),
                        cs(F32, RW_L, RW_L), cs(F32, RW_C, RW_L), cs(F32, RW_C, RW_L),
                        cs(F32, RW_C, RW_L), cs(F32, RW_C, RW_L), cs(F32, 8, RW_L)],
        compiler_params=_cparams(("parallel", "parallel")),
        name="rwkv_mix",
    )(pa, pa, pa, pa, pa, shift0, shift0, shift0, shift0, shift0, s0, mu, mu, mu, mu, mu,
      w0, w_up, a0, a_up, g_up1, g_up2, k_k, k_a, r_k, ln_w, ln_b)


LRU_ROWS = 64
LRU_L = 256


def _gelu_tanh(x):
    return 0.5 * x * (1.0 + jnp.tanh(np.sqrt(2.0 / np.pi).astype(np.float32) * (x + 0.044715 * (x * x * x))))


def lru_gates(c, wa_ref, ba, wx_ref, bx, lam):
    nb = c.shape[1] // B_BLOCK
    blocks = [c[:, i * B_BLOCK:(i + 1) * B_BLOCK] for i in range(nb)]
    gr = jnp.concatenate([_dot(blocks[i], wa_ref[i]) for i in range(nb)], axis=1) + ba
    gi = jnp.concatenate([_dot(blocks[i], wx_ref[i]) for i in range(nb)], axis=1) + bx
    log_a = -LRU_C * jax.nn.sigmoid(gr) * _softplus(-lam)
    a = jnp.exp(log_a)
    u = jnp.sqrt(-jnp.tanh(log_a) * (a * a + 1.0)) * jax.nn.sigmoid(gi) * c
    return a, u


def _lru_kernel(x_ref, g_ref, tail_ref, h0_ref, cw_ref, cb_ref, wa_ref, ba_ref, wx_ref, bx_ref, lam_ref, o_ref, h_ref, *,
                n_chunks, last_row, batch):
    @pl.when(pl.program_id(0) == batch)
    def _():
        o_ref[...] = jnp.zeros(o_ref.shape, o_ref.dtype)
        h_ref[...] = jnp.zeros(h_ref.shape, h_ref.dtype)

    @pl.when(pl.program_id(0) < batch)
    def _():
        _lru_body(x_ref, g_ref, tail_ref, h0_ref, cw_ref, cb_ref, wa_ref, ba_ref, wx_ref, bx_ref, lam_ref, o_ref, h_ref,
                  n_chunks=n_chunks, last_row=last_row)


def _lru_body(x_ref, g_ref, tail_ref, h0_ref, cw_ref, cb_ref, wa_ref, ba_ref, wx_ref, bx_ref, lam_ref, o_ref, h_ref, *,
              n_chunks, last_row):
    C, L = LRU_ROWS, LRU_L
    rows = lax.broadcasted_iota(jnp.int32, (C, L), 0)
    cw = cw_ref[...]

    def body(ci, carry):
        tail, h_prev, _ = carry
        r0 = pl.multiple_of(ci * C, C)
        x = x_ref[pl.ds(r0, C), :]
        xe = jnp.concatenate([tail, x], axis=0)
        c = cb_ref[...] + x * cw[3:4, :]
        for s in (1, 2, 3):
            c = c + pltpu.roll(xe, s, axis=0)[8:, :] * cw[3 - s:4 - s, :]
        a, u = lru_gates(c, wa_ref, ba_ref[...], wx_ref, bx_ref[...], lam_ref[...])
        d = 1
        while d < C:
            keep = rows >= d
            a_s = jnp.where(keep, pltpu.roll(a, d, axis=0), 1.0)
            u_s = jnp.where(keep, pltpu.roll(u, d, axis=0), 0.0)
            u = u + a * u_s
            a = a * a_s
            d *= 2
        h = a * h_prev + u
        o_ref[pl.ds(r0, C), :] = (h * _gelu_tanh(g_ref[pl.ds(r0, C), :])).astype(o_ref.dtype)
        return x[C - 8:, :], h[C - 1:C, :], h[last_row:last_row + 1, :]

    _, _, h_last = lax.fori_loop(0, n_chunks, body, (tail_ref[...], h0_ref[...], h0_ref[...]))
    h_ref[...] = h_last


def lru_mix(lru, tail0, h0, layer, conv_w, conv_b, w_a, b_a, w_x, b_x, lam, *, batch, seq, n_valid, out_rows):
    nj = B_WIDTH // LRU_L
    nb = LRU_L // B_BLOCK
    nc = seq // LRU_ROWS
    ng = batch + (out_rows > batch * seq)
    cb = lambda b: jnp.minimum(b, batch - 1)
    assert 0 <= n_valid - 1 - (nc - 1) * LRU_ROWS < LRU_ROWS
    vec = pl.BlockSpec((None, 1, LRU_L), lambda b, j: (layer, 0, j))
    wblk = pl.BlockSpec((None, nb, B_BLOCK, B_BLOCK), lambda b, j: (layer, j, 0, 0))
    return pl.pallas_call(
        functools.partial(_lru_kernel, n_chunks=nc, last_row=n_valid - 1 - (nc - 1) * LRU_ROWS, batch=batch),
        grid=(ng, nj),
        in_specs=[pl.BlockSpec((seq, LRU_L), lambda b, j: (cb(b), j)),
                  pl.BlockSpec((seq, LRU_L), lambda b, j: (cb(b), nj + j)),
                  pl.BlockSpec((None, 8, LRU_L), lambda b, j: (cb(b), 0, j)),
                  pl.BlockSpec((None, 1, LRU_L), lambda b, j: (cb(b), 0, j)),
                  pl.BlockSpec((None, 4, LRU_L), lambda b, j: (layer, 0, j)),
                  vec, wblk, vec, wblk, vec, vec],
        out_specs=[pl.BlockSpec((seq, LRU_L), lambda b, j: (b, j)),
                   pl.BlockSpec((None, 1, LRU_L), lambda b, j: (b, 0, j))],
        out_shape=[jax.ShapeDtypeStruct((out_rows, B_WIDTH), BF16),
                   jax.ShapeDtypeStruct((ng, 1, B_WIDTH), F32)],
        compiler_params=_cparams(("parallel", "parallel")),
        name="lru_mix",
    )(lru, lru, tail0, h0, conv_w, conv_b, w_a, b_a, w_x, b_x, lam)


C_BAND = 128
NEG_INF = float("-inf")


def _merge_groups(o_prev, lse_prev, o_new, lse_new):
    m = jnp.maximum(lse_prev, lse_new)
    wp = jnp.exp(lse_prev - m)
    wn = jnp.exp(lse_new - m)
    tot = wp + wn
    return (wp * o_prev + wn * o_new) / tot, m + jnp.log(tot)


def _attn_block(q, kcat, vcat, mask):
    sc = jnp.where(mask, _dot_nt(q, kcat) * C_SCALE, NEG_INF)
    m = jnp.max(sc, axis=-1, keepdims=True)
    p = jnp.exp(sc - m)
    den = jnp.sum(p, axis=-1, keepdims=True)
    o = _dot(p, vcat) / den
    return o, jnp.broadcast_to(m + jnp.log(den), o.shape)


def _attn_prompt_kernel(q0_ref, q1_ref, q2_ref, k0_ref, k1_ref, k2_ref, v0_ref, v1_ref, v2_ref, o_ref, oacc, lacc, *, seq,
                        batch):
    @pl.when(pl.program_id(0) == batch)
    def _():
        o_ref[...] = jnp.zeros(o_ref.shape, o_ref.dtype)

    @pl.when(pl.program_id(0) < batch)
    def _():
        _attn_prompt_body(q0_ref, q1_ref, q2_ref, k0_ref, k1_ref, k2_ref, v0_ref, v1_ref, v2_ref, o_ref, oacc, lacc, seq=seq)


def _attn_prompt_body(q0_ref, q1_ref, q2_ref, k0_ref, k1_ref, k2_ref, v0_ref, v1_ref, v2_ref, o_ref, oacc, lacc, *, seq):
    B = C_BAND
    iq = lax.broadcasted_iota(jnp.int32, (B, 2 * B), 0)
    jk = lax.broadcasted_iota(jnp.int32, (B, 2 * B), 1)
    band_ok = (jk >= iq) & (jk <= iq + B)
    iq1 = lax.broadcasted_iota(jnp.int32, (B, B), 0)
    jk1 = lax.broadcasted_iota(jnp.int32, (B, B), 1)
    causal = jk1 <= iq1
    refs = ((q0_ref, k0_ref, v0_ref), (q1_ref, k1_ref, v1_ref), (q2_ref, k2_ref, v2_ref))
    for gi, (_, dil) in enumerate(C_GROUPS):
        q_ref, k_ref, v_ref = refs[gi]
        n_blocks = seq // dil // B
        for r in range(dil):
            for s in range(n_blocks):
                rows = pl.ds(r + dil * B * s, B, stride=dil) if dil > 1 else pl.ds(B * s, B)
                q = q_ref[rows, :]
                if s == 0:
                    o, lse = _attn_block(q, k_ref[rows, :], v_ref[rows, :], causal)
                else:
                    keys = (pl.ds(r + dil * B * (s - 1), 2 * B, stride=dil) if dil > 1 else pl.ds(B * (s - 1), 2 * B))
                    o, lse = _attn_block(q, k_ref[keys, :], v_ref[keys, :], band_ok)
                if gi > 0:
                    o, lse = _merge_groups(oacc[rows, :], lacc[rows, :], o, lse)
                oacc[rows, :] = o
                if gi < C_N_GROUPS - 1:
                    lacc[rows, :] = lse
    o_ref[...] = oacc[...].astype(o_ref.dtype)


def attn_prompt(qn, kn, v, *, batch, seq):
    rows = qn.shape[0]
    nb = batch + (rows > batch * seq)
    spec = lambda gi: pl.BlockSpec((seq, C_HEAD_DIM), lambda b, h: (jnp.minimum(b, batch - 1), gi * C_HEADS + h))
    g = range(C_N_GROUPS)
    return pl.pallas_call(
        functools.partial(_attn_prompt_kernel, seq=seq, batch=batch),
        grid=(nb, C_HEADS),
        in_specs=[spec(gi) for gi in g] * 3,
        out_specs=pl.BlockSpec((seq, C_HEAD_DIM), lambda b, h: (b, h)),
        out_shape=jax.ShapeDtypeStruct((rows, C_GROUP_WIDTH), BF16),
        scratch_shapes=[pltpu.VMEM((seq, C_HEAD_DIM), F32), pltpu.VMEM((seq, C_HEAD_DIM), F32)],
        compiler_params=_cparams(("parallel", "parallel")),
        name="attn_prompt",
    )(qn, qn, qn, kn, kn, kn, v, v, v)


def _attn_sample_kernel(*refs, lb, dil, n_new, first):
    if first:
        q_ref, kn_ref, vn_ref, kc_ref, vc_ref, o_ref, l_ref = refs
    else:
        q_ref, kn_ref, vn_ref, kc_ref, vc_ref, oin_ref, lin_ref, o_ref, l_ref = refs
    R = q_ref.shape[0]
    qi = lax.broadcasted_iota(jnp.int32, (R, lb), 0)
    ci = lax.broadcasted_iota(jnp.int32, (R, lb), 1)
    off = lb + qi - ci
    cache_ok = (off % dil == 0) & (off <= C_BAND * dil)
    qn_i = lax.broadcasted_iota(jnp.int32, (R, R), 0)
    nn_i = lax.broadcasted_iota(jnp.int32, (R, R), 1)
    new_ok = (nn_i <= qn_i) & ((qn_i - nn_i) % dil == 0) & (nn_i < n_new)
    for h in range(C_HEADS):
        ls = slice(h * C_HEAD_DIM, (h + 1) * C_HEAD_DIM)
        q = q_ref[:, ls]
        sc = jnp.where(cache_ok, _dot_nt(q, kc_ref[:, h, :]) * C_SCALE, NEG_INF)
        sn = jnp.where(new_ok, _dot_nt(q, kn_ref[:, ls]) * C_SCALE, NEG_INF)
        m = jnp.maximum(jnp.max(sc, axis=-1, keepdims=True), jnp.max(sn, axis=-1, keepdims=True))
        pc = jnp.exp(sc - m)
        pn = jnp.exp(sn - m)
        den = jnp.sum(pc, axis=-1, keepdims=True) + jnp.sum(pn, axis=-1, keepdims=True)
        o = (_dot(pc, vc_ref[:, h, :]) + _dot(pn, vn_ref[:, ls])) / den
        lse = jnp.broadcast_to(m + jnp.log(den), o.shape)
        if not first:
            o, lse = _merge_groups(oin_ref[:, ls], lin_ref[:, ls], o, lse)
        o_ref[:, ls] = o
        l_ref[:, ls] = lse


def attn_sample_group(qn, kn, v, k_cache, v_cache, o_run, lse_run, *, layer, gi, dil, n_new, first):
    R = 8
    batch = qn.shape[0] // R
    lb = k_cache.shape[2]
    W = C_GROUP_WIDTH
    qmap = lambda b: (b, gi)
    omap = lambda b: (b, 0)
    cmap = lambda b: (layer, b, 0, 0, 0)
    in_specs = ([pl.BlockSpec((R, W), qmap)] * 3
                + [pl.BlockSpec((None, None, lb, C_HEADS, C_HEAD_DIM), cmap)] * 2)
    args = [qn, kn, v, k_cache, v_cache]
    if not first:
        in_specs += [pl.BlockSpec((R, W), omap)] * 2
        args += [o_run, lse_run]
    return pl.pallas_call(
        functools.partial(_attn_sample_kernel, lb=lb, dil=dil, n_new=n_new, first=first),
        grid=(batch,),
        in_specs=in_specs,
        out_specs=[pl.BlockSpec((R, W), omap)] * 2,
        out_shape=[jax.ShapeDtypeStruct((batch * R, W), F32)] * 2,
        compiler_params=_cparams(("parallel",)),
        name="attn_sample_g%d" % gi,
    )(*args)


def _cache_shift_kernel(a_ref, b_ref, f_ref, o_ref, *, n_new, n_blocks):
    tb = o_ref.shape[0]
    o_ref[0:tb - n_new] = a_ref[n_new:tb]

    @pl.when(pl.program_id(1) < n_blocks - 1)
    def _():
        o_ref[tb - n_new:tb] = b_ref[...]

    @pl.when(pl.program_id(1) == n_blocks - 1)
    def _():
        o_ref[tb - n_new:tb] = f_ref[...]


def cache_shift_append(cache, fresh):
    n, lb, hh, e = cache.shape
    n_new = fresh.shape[1]
    tb = min(lb, 512)
    nblk = lb // tb
    nxt = lambda i, j: (i, jnp.minimum((j + 1) * (tb // n_new), lb // n_new - 1), 0, 0)
    return pl.pallas_call(
        functools.partial(_cache_shift_kernel, n_new=n_new, n_blocks=nblk),
        grid=(n, nblk),
        in_specs=[pl.BlockSpec((None, tb, hh, e), lambda i, j: (i, j, 0, 0)),
                  pl.BlockSpec((None, n_new, hh, e), nxt),
                  pl.BlockSpec((None, n_new, hh, e), lambda i, j: (i, 0, 0, 0))],
        out_specs=pl.BlockSpec((None, tb, hh, e), lambda i, j: (i, j, 0, 0)),
        out_shape=jax.ShapeDtypeStruct(cache.shape, cache.dtype),
        compiler_params=_cparams(("parallel", "parallel")),
        name="cache_shift_append",
    )(cache, cache, fresh)


def _pad_last(x, n):
    return jnp.pad(x, [(0, 0)] * (x.ndim - 1) + [(0, n - x.shape[-1])])


def _rows_at(w, row0, rows):
    return jnp.pad(w, ((0, 0), (row0, rows - row0 - w.shape[1]), (0, 0)))


def _sample_rows(t, rows_per_batch):
    s = t[N_PROMPT:N_PROMPT + N_SAMPLE].reshape(DEC_BATCH, DEC_SEQ, t.shape[1])
    return jnp.pad(s, ((0, 0), (0, rows_per_batch - DEC_SEQ), (0, 0))).reshape(DEC_BATCH * rows_per_batch, t.shape[1])


def _with_tail(full, sample_out, rows_per_batch):
    w = full.shape[1]
    s = sample_out.reshape(DEC_BATCH, rows_per_batch, w)[:, :DEC_SEQ].reshape(N_SAMPLE, w).astype(full.dtype)
    tail = jnp.pad(s, ((0, M_PAD - N_PROMPT - N_SAMPLE), (0, 0)))
    return lax.dynamic_update_slice(full, tail, (N_PROMPT, 0))


def _state_to_blockdiag(s):
    b = s.shape[0]
    s5 = s.reshape(b, A_HEADS // 4, 4, A_HEAD_SIZE, A_HEAD_SIZE)
    return jnp.einsum('bgivk,ij->bgivjk', s5, jnp.eye(4, dtype=s.dtype)).reshape(b, A_HEADS // 4, RW_L, RW_L)


def _blockdiag_to_state(s):
    b = s.shape[0]
    s6 = s.reshape(b, A_HEADS // 4, 4, A_HEAD_SIZE, 4, A_HEAD_SIZE)
    return jnp.einsum('bgivik->bgivk', s6).reshape(b, A_HEADS, A_HEAD_SIZE, A_HEAD_SIZE)


def kernel(x_prompt, x_sample, state_rwkv_shift, state_rwkv_wkv, state_lru_conv, state_lru_h, cache_dil1_k, cache_dil1_v, cache_dil2_k, cache_dil2_v, cache_dil3_k, cache_dil3_v, norm_ffn1, ffn1_w_gate, ffn1_w_up, ffn1_w_down, norm_mix, w_in, rwkv_mu, rwkv_w0, rwkv_w_up, rwkv_a0, rwkv_a_up, rwkv_g_up, rwkv_k_k, rwkv_k_a, rwkv_r_k, rwkv_ln_w, rwkv_ln_b, lru_conv_w, lru_conv_b, lru_w_a, lru_b_a, lru_w_x, lru_b_x, lru_lambda, attn_q_norm, attn_k_norm, w_br_a, w_br_b, w_br_c, w_out, norm_ffn2, ffn2_w_gate, ffn2_w_up, ffn2_w_down):
    L = DEPTH
    caches = ((cache_dil1_k, cache_dil1_v), (cache_dil2_k, cache_dil2_v), (cache_dil3_k, cache_dil3_v))

    wd1, wd2 = ffn1_w_down.astype(BF16), ffn2_w_down.astype(BF16)
    w_rest = realign_cast(w_in)
    o = tuple(c - A_SHIFT_WIDTH for c in IN_OFFSETS)
    wba, wbb, wbc, wo = (w.astype(BF16) for w in (w_br_a, w_br_b, w_br_c, w_out))

    vec = lambda t: t.reshape(L, 1, -1)
    g1, gm, g2 = vec(norm_ffn1), vec(norm_mix), vec(norm_ffn2)
    zw0 = A_SPLIT_OFFSETS[2]
    za0, zg0 = A_SPLIT_OFFSETS[3] - zw0, A_SPLIT_OFFSETS[4] - zw0
    g_split = RW_L - zg0
    rw_params = (vec(_pad_last(rwkv_mu, PA_W)), vec(rwkv_w0), _rows_at(rwkv_w_up, 0, RW_L), vec(rwkv_a0),
                 _rows_at(rwkv_a_up, za0, RW_L), _rows_at(rwkv_g_up[:, :g_split], zg0, RW_L),
                 _rows_at(rwkv_g_up[:, g_split:], 0, RW_L), vec(rwkv_k_k), vec(rwkv_k_a), vec(rwkv_r_k),
                 vec(rwkv_ln_w), vec(rwkv_ln_b))
    lru_params = (lru_conv_w, vec(lru_conv_b), lru_w_a, vec(lru_b_a), lru_w_x, vec(lru_b_x), vec(lru_lambda))
    q_gain = attn_q_norm.reshape(L, C_N_GROUPS, 1, C_HEAD_DIM)
    k_gain = attn_k_norm.reshape(L, C_N_GROUPS, 1, C_HEAD_DIM)

    shift_p0 = jnp.zeros((BATCH, 1, PA_W), F32)
    wkv_p0 = jnp.zeros((BATCH, A_WIDTH // RW_L, RW_L, RW_L), F32)
    tail_p0 = jnp.zeros((BATCH, 8, B_WIDTH), F32)
    h_p0 = jnp.zeros((BATCH, 1, B_WIDTH), F32)

    x = jnp.concatenate([x_prompt.reshape(N_PROMPT, D_MODEL), x_sample.reshape(N_SAMPLE, D_MODEL),
                         jnp.zeros((M_PAD - N_PROMPT - N_SAMPLE, D_MODEL), F32)], axis=0)

    outs_p, outs_s = [], []
    for l in range(L):
        u = rmsnorm_rows(x, g1, l)
        hff = ffn_gateup(u, ffn1_w_gate, ffn1_w_up, l)
        h = matmul_resid(hff, wd1, x, l, 512, D_FF // 2, 0.5, "ffn1_down")
        u = rmsnorm_rows(h, gm, l)
        pa = matmul_fullk(u, w_in, l, 512, F32, "proj_rwkv", n=PA_W)
        lru = matmul_fullk(u, w_rest, l, 1024, F32, "proj_lru", col0=o[0], n=2 * B_WIDTH)
        qn = matmul_qknorm(u, w_rest, q_gain, l, "proj_q", col0=o[2])
        kn = matmul_qknorm(u, w_rest, k_gain, l, "proj_k", col0=o[3])
        vv = matmul_fullk(u, w_rest, l, 1024, F32, "proj_v", col0=o[4], n=C_QKV_WIDTH)
        gates = matmul_fullk(u, w_rest, l, 1024, F32, "proj_gates", col0=o[5], n=3 * D_MODEL)

        oa, wkv_p = rwkv_mix(pa, shift_p0, wkv_p0, l, *rw_params, batch=BATCH, seq=SEQ, n_valid=SEQ, out_rows=M_PAD)
        oa_s, wkv_s = rwkv_mix(_sample_rows(pa, RW_C), _pad_last(state_rwkv_shift[l], PA_W)[:, None, :],
                               _state_to_blockdiag(state_rwkv_wkv[l]), l, *rw_params, batch=DEC_BATCH, seq=RW_C,
                               n_valid=DEC_SEQ, out_rows=DEC_BATCH * RW_C)
        oa = _with_tail(oa, oa_s, RW_C)
        shift_p = pa[SEQ - 1:N_PROMPT:SEQ]
        shift_s = pa[N_PROMPT + DEC_SEQ - 1:N_PROMPT + N_SAMPLE:DEC_SEQ]

        ob, lh_p = lru_mix(lru, tail_p0, h_p0, l, *lru_params, batch=BATCH, seq=SEQ, n_valid=SEQ, out_rows=M_PAD)
        tail_s = jnp.pad(state_lru_conv[l], ((0, 0), (8 - (CONV_WIDTH - 1), 0), (0, 0)))
        ob_s, lh_s = lru_mix(_sample_rows(lru, LRU_ROWS), tail_s, state_lru_h[l][:, None, :], l, *lru_params,
                             batch=DEC_BATCH, seq=LRU_ROWS, n_valid=DEC_SEQ, out_rows=DEC_BATCH * LRU_ROWS)
        ob = _with_tail(ob, ob_s, LRU_ROWS)
        lx_s = lru[N_PROMPT:N_PROMPT + N_SAMPLE, :B_WIDTH].reshape(DEC_BATCH, DEC_SEQ, B_WIDTH)
        conv_p = jnp.stack([lru[(b + 1) * SEQ - (CONV_WIDTH - 1):(b + 1) * SEQ, :B_WIDTH] for b in range(BATCH)])
        conv_s = jnp.concatenate([state_lru_conv[l], lx_s], axis=1)[:, -(CONV_WIDTH - 1):]

        oc = attn_prompt(qn, kn, vv, batch=BATCH, seq=SEQ)
        qs, ks, vs = _sample_rows(qn, 8), _sample_rows(kn, 8), _sample_rows(vv, 8)
        o_run = lse_run = None
        for gi, (window, dil) in enumerate(C_GROUPS):
            kc, vc = caches[gi]
            o_run, lse_run = attn_sample_group(qs, ks, vs, kc, vc, o_run, lse_run, layer=l, gi=gi, dil=dil,
                                               n_new=DEC_SEQ, first=gi == 0)
        oc = _with_tail(oc, o_run, 8)

        new_p = [shift_p[:, :A_SHIFT_WIDTH], _blockdiag_to_state(wkv_p[:BATCH]), conv_p, lh_p[:BATCH, 0]]
        new_s = [shift_s[:, :A_SHIFT_WIDTH], _blockdiag_to_state(wkv_s), conv_s, lh_s[:, 0]]
        for gi, (window, _) in enumerate(C_GROUPS):
            keep = min(window, SEQ)
            cols = slice(gi * C_GROUP_WIDTH, (gi + 1) * C_GROUP_WIDTH)
            for t, cache in ((kn, caches[gi][0]), (vv, caches[gi][1])):
                kept = jnp.stack([t[(b + 1) * SEQ - keep:(b + 1) * SEQ, cols] for b in range(BATCH)])
                new_p.append(kept.reshape(BATCH, keep, C_HEADS, C_HEAD_DIM))
                new_s.append(t[N_PROMPT:N_PROMPT + N_SAMPLE, cols].reshape(DEC_BATCH, DEC_SEQ, C_HEADS, C_HEAD_DIM))
        outs_p.append(new_p)
        outs_s.append(new_s)

        merged = branch_merge(oa, ob, oc, wba, wbb, wbc, gates, l)
        h = matmul_resid(merged, wo, h, l, 512, D_MODEL, 1.0, "w_out")
        u = rmsnorm_rows(h, g2, l)
        hff = ffn_gateup(u, ffn2_w_gate, ffn2_w_up, l)
        x = matmul_resid(hff, wd2, h, l, 512, D_FF // 2, 0.5, "ffn2_down")

    res = [x[:N_PROMPT].reshape(BATCH, SEQ, D_MODEL),
           x[N_PROMPT:N_PROMPT + N_SAMPLE].reshape(DEC_BATCH, DEC_SEQ, D_MODEL)]
    flat_caches = [c for pair in caches for c in pair]
    for i in range(10):
        res.append(jnp.stack([o_[i] for o_ in outs_p]))
        s_i = jnp.stack([o_[i] for o_ in outs_s])
        if i >= 4:
            cache = flat_caches[i - 4]
            lb = cache.shape[2]
            s_i = cache_shift_append(cache.reshape(L * DEC_BATCH, lb, C_HEADS, C_HEAD_DIM),
                                     s_i.reshape(L * DEC_BATCH, DEC_SEQ, C_HEADS, C_HEAD_DIM)).reshape(cache.shape)
        res.append(s_i)
    return tuple(res)
```

```python
import functools

import numpy as np
import jax
import jax.numpy as jnp
from jax import lax
from jax.experimental import pallas as pl
from jax.experimental.pallas import tpu as pltpu

F32 = jnp.float32
BF16 = jnp.bfloat16

D_MODEL = 4096
BATCH = 4
SEQ = 2048
DEPTH = 4
DEC_BATCH = 8
DEC_SEQ = 4
PAST_LEN = 8192

A_HEAD_SIZE = 64
A_WIDTH = D_MODEL // 2
A_HEADS = A_WIDTH // A_HEAD_SIZE
A_DECAY_RANK = 96
A_ICLR_RANK = 96
A_GATE_RANK = 256
A_GN_EPS = 64e-5
A_SHIFT_WIDTH = 3 * A_WIDTH + A_DECAY_RANK + A_ICLR_RANK + A_GATE_RANK
A_SPLIT_OFFSETS = (A_WIDTH, 2 * A_WIDTH, 3 * A_WIDTH, 3 * A_WIDTH + A_DECAY_RANK,
                   3 * A_WIDTH + A_DECAY_RANK + A_ICLR_RANK)
B_WIDTH = D_MODEL // 2
B_BLOCKS = 16
B_BLOCK = B_WIDTH // B_BLOCKS
CONV_WIDTH = 4
LRU_C = 8.0
C_HEADS = 8
C_HEAD_DIM = 128
C_GROUPS = ((128, 1), (512, 4), (2048, 16))
C_N_GROUPS = 3
C_GROUP_WIDTH = C_HEADS * C_HEAD_DIM
C_QKV_WIDTH = C_N_GROUPS * C_GROUP_WIDTH
C_SCALE = C_HEAD_DIM ** -0.5
QK_EPS = 1e-6
D_FF = 11008
NORM_EPS = 1e-6
IN_SPLITS = (A_SHIFT_WIDTH, B_WIDTH, B_WIDTH, C_QKV_WIDTH, C_QKV_WIDTH, C_QKV_WIDTH, D_MODEL, D_MODEL, D_MODEL)
IN_OFFSETS = tuple(int(s) for s in np.cumsum(IN_SPLITS)[:-1])

LANE = 128
VMEM_LIMIT_BYTES = 56 * 1024 * 1024

N_PROMPT = BATCH * SEQ
N_SAMPLE = DEC_BATCH * DEC_SEQ
ROW_TILE = 1040
M_PAD = 8 * ROW_TILE
A_SHIFT_PAD = 13 * 512

assert M_PAD >= N_PROMPT + N_SAMPLE


def _cparams(sem):
    return pltpu.CompilerParams(dimension_semantics=sem, vmem_limit_bytes=VMEM_LIMIT_BYTES)


def _rmsnorm_kernel(x_ref, g_ref, o_ref):
    x = x_ref[...]
    ms = jnp.mean(x * x, axis=-1, keepdims=True)
    o_ref[...] = (x * lax.rsqrt(ms + NORM_EPS) * g_ref[...]).astype(o_ref.dtype)


def rmsnorm_rows(x, g, layer):
    m, d = x.shape
    tm = 208
    return pl.pallas_call(
        _rmsnorm_kernel,
        grid=(m // tm,),
        in_specs=[pl.BlockSpec((tm, d), lambda i: (i, 0)),
                  pl.BlockSpec((None, 1, d), lambda i: (layer, 0, 0))],
        out_specs=pl.BlockSpec((tm, d), lambda i: (i, 0)),
        out_shape=jax.ShapeDtypeStruct((m, d), BF16),
        compiler_params=_cparams(("parallel",)),
        name="rmsnorm",
    )(x, g)


def _dot_wt(x, wt):
    return lax.dot_general(x, wt.astype(BF16), (((1,), (1,)), ((), ())), preferred_element_type=F32)


def _mm_kernel(x_ref, w_ref, o_ref):
    o_ref[...] = _dot_wt(x_ref[...], w_ref[...]).astype(o_ref.dtype)


def matmul_fullk(x, wt, layer, tn, out_dtype, name, col0=0, n=None):
    m, k = x.shape
    n = wt.shape[1] if n is None else n
    c0 = col0 // tn
    assert c0 * tn == col0 and n % tn == 0
    return pl.pallas_call(
        _mm_kernel,
        grid=(m // ROW_TILE, n // tn),
        in_specs=[pl.BlockSpec((ROW_TILE, k), lambda i, j: (i, 0)),
                  pl.BlockSpec((None, tn, k), lambda i, j: (layer, c0 + j, 0))],
        out_specs=pl.BlockSpec((ROW_TILE, tn), lambda i, j: (i, j)),
        out_shape=jax.ShapeDtypeStruct((m, n), out_dtype),
        compiler_params=_cparams(("parallel", "parallel")),
        name=name,
    )(x, wt)


def _qknorm_mm_kernel(x_ref, w_ref, g_ref, o_ref):
    acc = _dot_wt(x_ref[...], w_ref[...])
    gain = g_ref[...]
    for h in range(acc.shape[1] // C_HEAD_DIM):
        ls = slice(h * C_HEAD_DIM, (h + 1) * C_HEAD_DIM)
        t = acc[:, ls]
        o_ref[:, ls] = t * lax.rsqrt(jnp.mean(t * t, axis=-1, keepdims=True) + QK_EPS) * gain


def matmul_qknorm(x, wt, gain, layer, name, col0):
    m, k = x.shape
    n = C_QKV_WIDTH
    tn = C_GROUP_WIDTH
    c0 = col0 // tn
    assert c0 * tn == col0
    return pl.pallas_call(
        _qknorm_mm_kernel,
        grid=(m // ROW_TILE, n // tn),
        in_specs=[pl.BlockSpec((ROW_TILE, k), lambda i, j: (i, 0)),
                  pl.BlockSpec((None, tn, k), lambda i, j: (layer, c0 + j, 0)),
                  pl.BlockSpec((None, None, 1, C_HEAD_DIM), lambda i, j: (layer, j, 0, 0))],
        out_specs=pl.BlockSpec((ROW_TILE, tn), lambda i, j: (i, j)),
        out_shape=jax.ShapeDtypeStruct((m, n), F32),
        compiler_params=_cparams(("parallel", "parallel")),
        name=name,
    )(x, wt, gain)


W_REST_ROWS = 512

assert A_SHIFT_WIDTH % 64 == 0 and (sum(IN_SPLITS) - A_SHIFT_WIDTH) % W_REST_ROWS == 0


def _cast_rows_kernel(a_ref, o_ref):
    o_ref[...] = a_ref[0].astype(o_ref.dtype)


def cast_rows(wt, row0):
    nl, n, k = wt.shape
    rows = n - row0
    return pl.pallas_call(
        _cast_rows_kernel,
        grid=(nl, rows // W_REST_ROWS),
        in_specs=[pl.BlockSpec((pl.Element(1), pl.Element(W_REST_ROWS), pl.Element(k)),
                               lambda l, j: (l, pl.multiple_of(row0 + j * W_REST_ROWS, 64), 0))],
        out_specs=pl.BlockSpec((None, W_REST_ROWS, k), lambda l, j: (l, j, 0)),
        out_shape=jax.ShapeDtypeStruct((nl, rows, k), BF16),
        compiler_params=_cparams(("parallel", "parallel")),
        name="cast_rows",
    )(wt)


def _gateup_kernel(x_ref, wg_ref, wu_ref, o_ref):
    x = x_ref[...]
    g = jnp.dot(x, wg_ref[...].astype(BF16), preferred_element_type=F32)
    u = jnp.dot(x, wu_ref[...].astype(BF16), preferred_element_type=F32)
    o_ref[...] = (g * jax.nn.sigmoid(g) * u).astype(o_ref.dtype)


def ffn_gateup(x, wg, wu, layer):
    m, k = x.shape
    n = wg.shape[2]
    tn = 512
    return pl.pallas_call(
        _gateup_kernel,
        grid=(m // ROW_TILE, pl.cdiv(n, tn)),
        in_specs=[pl.BlockSpec((ROW_TILE, k), lambda i, j: (i, 0), pipeline_mode=pl.Buffered(1)),
                  pl.BlockSpec((None, k, tn), lambda i, j: (layer, 0, j)),
                  pl.BlockSpec((None, k, tn), lambda i, j: (layer, 0, j))],
        out_specs=pl.BlockSpec((ROW_TILE, tn), lambda i, j: (i, j)),
        out_shape=jax.ShapeDtypeStruct((m, n), BF16),
        compiler_params=_cparams(("parallel", "parallel")),
        name="ffn_gateup",
    )(x, wg, wu)


def _mm_resid_kernel(x_ref, w_ref, r_ref, o_ref, *, scale, k_total):
    kk = pl.program_id(2)
    tk = x_ref.shape[1]
    n_k = pl.cdiv(k_total, tk)

    def accumulate(x, w):
        p = scale * jnp.dot(x, w, preferred_element_type=F32)

        @pl.when(kk == 0)
        def _():
            o_ref[...] = r_ref[...] + p

        @pl.when(kk > 0)
        def _():
            o_ref[...] += p

    if k_total % tk == 0:
        accumulate(x_ref[...], w_ref[...])
    else:
        @pl.when(kk < n_k - 1)
        def _():
            accumulate(x_ref[...], w_ref[...])

        @pl.when(kk == n_k - 1)
        def _():
            valid = k_total - (n_k - 1) * tk
            x, w = x_ref[...], w_ref[...]
            x = jnp.where(lax.broadcasted_iota(jnp.int32, x.shape, 1) < valid, x, jnp.zeros((), x.dtype))
            w = jnp.where(lax.broadcasted_iota(jnp.int32, w.shape, 0) < valid, w, jnp.zeros((), w.dtype))
            accumulate(x, w)


def matmul_resid(x, w, resid, layer, tn, tk, scale, name):
    m, k = x.shape
    n = w.shape[2]
    return pl.pallas_call(
        functools.partial(_mm_resid_kernel, scale=scale, k_total=k),
        grid=(m // ROW_TILE, n // tn, pl.cdiv(k, tk)),
        in_specs=[pl.BlockSpec((ROW_TILE, tk), lambda i, j, kk: (i, kk)),
                  pl.BlockSpec((None, tk, tn), lambda i, j, kk: (layer, kk, j)),
                  pl.BlockSpec((ROW_TILE, tn), lambda i, j, kk: (i, j))],
        out_specs=pl.BlockSpec((ROW_TILE, tn), lambda i, j, kk: (i, j)),
        out_shape=jax.ShapeDtypeStruct((m, n), F32),
        compiler_params=_cparams(("parallel", "parallel", "arbitrary")),
        name=name,
    )(x, w, resid)


def _merge_kernel(oa_ref, ob_ref, oc_ref, wa_ref, wb_ref, wc_ref, ga_ref, gb_ref, gc_ref, o_ref):
    a = jnp.dot(oa_ref[...], wa_ref[...], preferred_element_type=F32)
    b = jnp.dot(ob_ref[...], wb_ref[...], preferred_element_type=F32)
    c = jnp.dot(oc_ref[...], wc_ref[...], preferred_element_type=F32)
    o_ref[...] = (jax.nn.sigmoid(ga_ref[...]) * a + jax.nn.sigmoid(gb_ref[...]) * b
                  + jax.nn.sigmoid(gc_ref[...]) * c).astype(o_ref.dtype)


def branch_merge(oa, ob, oc, wa, wb, wc, gates, layer):
    m = oa.shape[0]
    n = wa.shape[2]
    tn = 512
    nb = n // tn
    row = lambda i, j: (i, 0)
    wcol = lambda i, j: (layer, 0, j)
    return pl.pallas_call(
        _merge_kernel,
        grid=(m // ROW_TILE, nb),
        in_specs=[pl.BlockSpec((ROW_TILE, oa.shape[1]), row),
                  pl.BlockSpec((ROW_TILE, ob.shape[1]), row),
                  pl.BlockSpec((ROW_TILE, oc.shape[1]), row),
                  pl.BlockSpec((None, wa.shape[1], tn), wcol),
                  pl.BlockSpec((None, wb.shape[1], tn), wcol),
                  pl.BlockSpec((None, wc.shape[1], tn), wcol),
                  pl.BlockSpec((ROW_TILE, tn), lambda i, j: (i, j)),
                  pl.BlockSpec((ROW_TILE, tn), lambda i, j: (i, nb + j)),
                  pl.BlockSpec((ROW_TILE, tn), lambda i, j: (i, 2 * nb + j))],
        out_specs=pl.BlockSpec((ROW_TILE, tn), lambda i, j: (i, j)),
        out_shape=jax.ShapeDtypeStruct((m, n), BF16),
        compiler_params=_cparams(("parallel", "parallel")),
        name="branch_merge",
    )(oa, ob, oc, wa, wb, wc, gates, gates, gates)


RW_C = 64
RW_L = 256
RW_HS = A_HEAD_SIZE
RW_UNROLL = 8
PA_W = A_SHIFT_PAD


def _dot(a, b):
    return jnp.dot(a.astype(BF16), b.astype(BF16), preferred_element_type=F32)


def _dot_nt(a, b):
    return lax.dot_general(a.astype(BF16), b.astype(BF16), (((1,), (1,)), ((), ())), preferred_element_type=F32)


def _dot_tn(a, b):
    return lax.dot_general(a.astype(BF16), b.astype(BF16), (((0,), (0,)), ((), ())), preferred_element_type=F32)


def _split_dot(e, x, parts):
    acc = None
    rem = x
    for _ in range(parts):
        p = rem.astype(BF16)
        rem = rem - p.astype(F32)
        d = jnp.dot(e, p, preferred_element_type=F32)
        acc = d if acc is None else acc + d
    return acc


def _split_dot_r(x, e, parts):
    acc = None
    rem = x
    for _ in range(parts):
        p = rem.astype(BF16)
        rem = rem - p.astype(F32)
        d = jnp.dot(p, e, preferred_element_type=F32)
        acc = d if acc is None else acc + d
    return acc


def _softplus(x):
    return jnp.maximum(x, 0.0) + jnp.log1p(jnp.exp(-jnp.abs(x)))


def _interleave(gens):
    results = [None] * len(gens)
    live = list(range(len(gens)))
    while live:
        for i in list(live):
            try:
                next(gens[i])
            except StopIteration as done:
                results[i] = done.value
                live.remove(i)
    return results


def rwkv_prep(r, k, v, z1, z2, w0, w_up, a0, a_up, g_up1, g_up2, k_k, k_a, r_k, seg):
    logw = -_softplus(-(w0 + _dot(jnp.tanh(z1), w_up))) - 0.5
    ld = -jnp.exp(logw)
    a = jax.nn.sigmoid(a0 + _dot(z1, a_up))
    g = _dot(jax.nn.sigmoid(z1), g_up1) + _dot(jax.nn.sigmoid(z2), g_up2)
    kk = k * k_k
    ss = _split_dot_r(kk * kk, seg, 2)
    yield
    kk = kk / jnp.maximum(jnp.sqrt(ss), 1e-12)
    k_h = k * (1.0 + (a - 1.0) * k_a)
    bonus = _split_dot_r(r * k_h * r_k, seg, 2) * v
    yield
    return ld, -kk, kk * a, k_h, g, bonus


def _rwkv_kernel(r_ref, k_ref, v_ref, wa_ref, zg_ref, sh_r, sh_k, sh_v, sh_wa, sh_g, s0_ref,
                 mu_r, mu_k, mu_v, mu_wa, mu_g,
                 w0_ref, wup_ref, a0_ref, aup_ref, gup1_ref, gup2_ref, kk_ref, ka_ref, rk_ref, lnw_ref, lnb_ref,
                 o_ref, s_ref,
                 at_s, rh_s, nrb_s, gm_s, hm_s, uv_s, yv_s, bon_s, g_s, gc_s, *, n_chunks, n_valid, unroll, batch):
    @pl.when(pl.program_id(0) == batch)
    def _():
        o_ref[...] = jnp.zeros(o_ref.shape, o_ref.dtype)
        s_ref[...] = jnp.zeros(s_ref.shape, s_ref.dtype)

    @pl.when(pl.program_id(0) < batch)
    def _():
        _rwkv_body(r_ref, k_ref, v_ref, wa_ref, zg_ref, sh_r, sh_k, sh_v, sh_wa, sh_g, s0_ref, mu_r, mu_k, mu_v, mu_wa, mu_g,
                   w0_ref, wup_ref, a0_ref, aup_ref, gup1_ref, gup2_ref, kk_ref, ka_ref, rk_ref, lnw_ref, lnb_ref, o_ref, s_ref,
                   at_s, rh_s, nrb_s, gm_s, hm_s, uv_s, yv_s, bon_s, g_s, gc_s, n_chunks=n_chunks, n_valid=n_valid,
                   unroll=unroll)


def _rwkv_body(r_ref, k_ref, v_ref, wa_ref, zg_ref, sh_r, sh_k, sh_v, sh_wa, sh_g, s0_ref, mu_r, mu_k, mu_v, mu_wa, mu_g,
               w0_ref, wup_ref, a0_ref, aup_ref, gup1_ref, gup2_ref, kk_ref, ka_ref, rk_ref, lnw_ref, lnb_ref, o_ref, s_ref,
               at_s, rh_s, nrb_s, gm_s, hm_s, uv_s, yv_s, bon_s, g_s, gc_s, *, n_chunks, n_valid, unroll):
    C, L = RW_C, RW_L
    row = lax.broadcasted_iota(jnp.int32, (C, 4 * C), 0)
    col = lax.broadcasted_iota(jnp.int32, (C, 4 * C), 1) % C
    strict = col < row
    incl = col <= row
    eye_cat = (col == row).astype(F32)
    bd_mask = (lax.broadcasted_iota(jnp.int32, (L, L), 0) // RW_HS) == (lax.broadcasted_iota(jnp.int32, (L, L), 1) // RW_HS)
    seg = bd_mask.astype(BF16)
    ltri = (lax.broadcasted_iota(jnp.int32, (C, C), 1) <= lax.broadcasted_iota(jnp.int32, (C, C), 0)).astype(BF16)
    first_row = lax.broadcasted_iota(jnp.int32, (C, L), 0) == 0

    def stack4(z):
        zb = z.astype(BF16)
        return jnp.where(bd_mask, jnp.concatenate([zb, zb, zb, zb], axis=0), jnp.zeros((), BF16))

    def shifted(x_ref, mu_ref, r0, last):
        x = x_ref[pl.ds(r0, C), :]
        prev = jnp.where(first_row, last, pltpu.roll(x, 1, axis=0))
        return x + (prev - x) * mu_ref[...], x[C - 1:C, :]

    def chunk_terms(xr, xk, xv, xwa, xg):
        ld, a_vec, b_vec, k_h, g, bonus = yield from rwkv_prep(
            xr, xk, xv, xwa, xg, w0_ref[...], wup_ref[...], a0_ref[...], aup_ref[...],
            gup1_ref[...], gup2_ref[...], kk_ref[...], ka_ref[...], rk_ref[...], seg)
        if n_valid < n_chunks * C:
            live = lax.broadcasted_iota(jnp.int32, (C, L), 0) < n_valid
            ld, a_vec, b_vec, k_h = (jnp.where(live, t_, 0.0) for t_ in (ld, a_vec, b_vec, k_h))
        cum = _split_dot(ltri, ld, 3)
        yield
        cum_c = cum[C - 1:C, :]
        ah = a_vec * jnp.exp(cum - ld)
        rh = xr * jnp.exp(cum)
        inv = jnp.exp(-cum)
        rel = jnp.exp(cum_c - cum)
        ar = jnp.concatenate([ah, rh], axis=0)
        pb = _dot_nt(ar, stack4(b_vec * inv))
        pk = _dot_nt(ar, stack4(k_h * inv))
        yield
        nab = jnp.where(strict, pb[:C], 0.0)
        nrb = jnp.where(incl, pb[C:], 0.0)
        nak = jnp.where(strict, pk[:C], 0.0)
        nrk = jnp.where(incl, pk[C:], 0.0)
        t = eye_cat + nab
        n = _dot(nab, stack4(nab))
        nakv = _dot(nak, stack4(xv))
        yv = _dot(nrk, stack4(xv))
        yield
        for _ in range(5):
            p = _dot(jnp.concatenate([n, t], axis=0), stack4(n))
            yield
            t = t + p[C:]
            n = p[:C]
        at = _dot(t, stack4(ah))
        uv = _dot(t, stack4(nakv))
        yield
        bt = (b_vec * rel).astype(BF16)
        kt = (k_h * rel).astype(BF16)
        gm = _dot_tn(at, bt)
        hm = _dot_tn(jnp.concatenate([uv.astype(BF16), xv.astype(BF16)], axis=0), jnp.concatenate([bt, kt], axis=0))
        yield
        return (at.astype(BF16), rh.astype(BF16), nrb.astype(BF16), jnp.where(bd_mask, gm, 0.0).astype(BF16),
                jnp.where(bd_mask, hm, 0.0), uv, yv, bonus, g, jnp.broadcast_to(jnp.exp(cum_c), (8, L)))

    scratch = (at_s, rh_s, nrb_s, gm_s, hm_s, uv_s, yv_s, bon_s, g_s, gc_s)
    in_refs = ((r_ref, mu_r), (k_ref, mu_k), (v_ref, mu_v), (wa_ref, mu_wa), (zg_ref, mu_g))

    def phase1(i, carry):
        gens = []
        for j in range(unroll):
            r0 = pl.multiple_of((i * unroll + j) * C, C)
            xs = []
            nxt = []
            for (x_ref, mu_ref), last in zip(in_refs, carry):
                xm, tail = shifted(x_ref, mu_ref, r0, last)
                xs.append(xm)
                nxt.append(tail)
            carry = tuple(nxt)
            gens.append(chunk_terms(*xs))
        for j, vals in enumerate(_interleave(gens)):
            for ref, val in zip(scratch, vals):
                ref[i * unroll + j] = val
        return carry

    lax.fori_loop(0, n_chunks // unroll, phase1, (sh_r[...], sh_k[...], sh_v[...], sh_wa[...], sh_g[...]))

    def chunk_out(c, sb):
        u = _dot_nt(at_s[c], sb) + uv_s[c]
        ys = _dot_nt(rh_s[c], sb)
        yield
        y = ys + _dot(nrb_s[c], stack4(u)) + yv_s[c]
        yield
        mean = _split_dot_r(y, seg, 2) * (1.0 / RW_HS)
        yield
        d = y - mean
        var = _split_dot_r(d * d, seg, 2) * (1.0 / RW_HS)
        yield
        yn = d * lax.rsqrt(var + A_GN_EPS) * lnw_ref[...] + lnb_ref[...]
        return ((yn + bon_s[c]) * g_s[c]).astype(o_ref.dtype)

    def phase2(i, sbd):
        gens = []
        for j in range(unroll):
            c = i * unroll + j
            sb = sbd.astype(BF16)
            gens.append(chunk_out(c, sb))
            sbd = gc_s[c][0:1, :] * sbd + _dot(sb, gm_s[c]) + hm_s[c]
        for j, o in enumerate(_interleave(gens)):
            r0 = pl.multiple_of((i * unroll + j) * C, C)
            o_ref[pl.ds(r0, C), :] = o
        return sbd

    s_ref[...] = lax.fori_loop(0, n_chunks // unroll, phase2, s0_ref[...])


def rwkv_mix(pa, shift0, s0, layer, mu, w0, w_up, a0, a_up, g_up1, g_up2, k_k, k_a, r_k, ln_w, ln_b, *, batch, seq,
             n_valid, out_rows):
    n_hg = A_WIDTH // RW_L
    nc = seq // RW_C
    nb = batch + (out_rows > batch * seq)
    cb = lambda b: jnp.minimum(b, batch - 1)
    col = lambda off: (lambda b, h: (cb(b), off + h))
    fixed = lambda off: (lambda b, h: (cb(b), off))
    pcol = lambda off: (lambda b, h: (layer, 0, off + h))
    pfix = lambda off: (lambda b, h: (layer, 0, off))
    scol = lambda off: (lambda b, h: (cb(b), 0, off + h))
    sfix = lambda off: (lambda b, h: (cb(b), 0, off))
    blk = lambda im: pl.BlockSpec((seq, RW_L), im)
    pblk = lambda rows, im: pl.BlockSpec((None, rows, RW_L), im)
    cs = lambda dt, *shape: pltpu.VMEM((nc,) + shape, dt)
    return pl.pallas_call(
        functools.partial(_rwkv_kernel, n_chunks=nc, n_valid=n_valid, unroll=min(RW_UNROLL, nc), batch=batch),
        grid=(nb, n_hg),
        in_specs=[blk(col(0)), blk(col(n_hg)), blk(col(2 * n_hg)), blk(fixed(3 * n_hg)), blk(fixed(3 * n_hg + 1)),
                  pblk(1, scol(0)), pblk(1, scol(n_hg)), pblk(1, scol(2 * n_hg)), pblk(1, sfix(3 * n_hg)),
                  pblk(1, sfix(3 * n_hg + 1)),
                  pl.BlockSpec((None, None, RW_L, RW_L), lambda b, h: (cb(b), h, 0, 0)),
                  pblk(1, pcol(0)), pblk(1, pcol(n_hg)), pblk(1, pcol(2 * n_hg)), pblk(1, pfix(3 * n_hg)),
                  pblk(1, pfix(3 * n_hg + 1)),
                  pblk(1, pcol(0)), pblk(RW_L, pcol(0)), pblk(1, pcol(0)), pblk(RW_L, pcol(0)), pblk(RW_L, pcol(0)),
                  pblk(RW_L, pcol(0)), pblk(1, pcol(0)), pblk(1, pcol(0)), pblk(1, pcol(0)), pblk(1, pcol(0)),
                  pblk(1, pcol(0))],
        out_specs=[pl.BlockSpec((seq, RW_L), lambda b, h: (b, h)),
                   pl.BlockSpec((None, None, RW_L, RW_L), lambda b, h: (b, h, 0, 0))],
        out_shape=[jax.ShapeDtypeStruct((out_rows, A_WIDTH), BF16),
                   jax.ShapeDtypeStruct((nb, n_hg, RW_L, RW_L), F32)],
        scratch_shapes=[cs(BF16, RW_C, RW_L), cs(BF16, RW_C, RW_L), cs(BF16, RW_C, 4 * RW_C), cs(BF16, RW_L, RW_L),
                        cs(F32, RW_L, RW_L), cs(F32, RW_C, RW_L), cs(F32, RW_C, RW_L),
                        cs(F32, RW_C, RW_L), cs(F32, RW_C, RW_L), cs(F32, 8, RW_L)],
        compiler_params=_cparams(("parallel", "parallel")),
        name="rwkv_mix",
    )(pa, pa, pa, pa, pa, shift0, shift0, shift0, shift0, shift0, s0, mu, mu, mu, mu, mu,
      w0, w_up, a0, a_up, g_up1, g_up2, k_k, k_a, r_k, ln_w, ln_b)


LRU_ROWS = 64
LRU_L = 256


def _gelu_tanh(x):
    return 0.5 * x * (1.0 + jnp.tanh(np.sqrt(2.0 / np.pi).astype(np.float32) * (x + 0.044715 * (x * x * x))))


def lru_gates(c, wa_ref, ba, wx_ref, bx, lam):
    nb = c.shape[1] // B_BLOCK
    blocks = [c[:, i * B_BLOCK:(i + 1) * B_BLOCK] for i in range(nb)]
    gr = jnp.concatenate([_dot(blocks[i], wa_ref[i]) for i in range(nb)], axis=1) + ba
    gi = jnp.concatenate([_dot(blocks[i], wx_ref[i]) for i in range(nb)], axis=1) + bx
    log_a = -LRU_C * jax.nn.sigmoid(gr) * _softplus(-lam)
    a = jnp.exp(log_a)
    u = jnp.sqrt(-jnp.tanh(log_a) * (a * a + 1.0)) * jax.nn.sigmoid(gi) * c
    return a, u


def _lru_kernel(x_ref, g_ref, tail_ref, h0_ref, cw_ref, cb_ref, wa_ref, ba_ref, wx_ref, bx_ref, lam_ref, o_ref, h_ref, *,
                n_chunks, last_row, batch):
    @pl.when(pl.program_id(0) == batch)
    def _():
        o_ref[...] = jnp.zeros(o_ref.shape, o_ref.dtype)
        h_ref[...] = jnp.zeros(h_ref.shape, h_ref.dtype)

    @pl.when(pl.program_id(0) < batch)
    def _():
        _lru_body(x_ref, g_ref, tail_ref, h0_ref, cw_ref, cb_ref, wa_ref, ba_ref, wx_ref, bx_ref, lam_ref, o_ref, h_ref,
                  n_chunks=n_chunks, last_row=last_row)


def _lru_body(x_ref, g_ref, tail_ref, h0_ref, cw_ref, cb_ref, wa_ref, ba_ref, wx_ref, bx_ref, lam_ref, o_ref, h_ref, *,
              n_chunks, last_row):
    C, L = LRU_ROWS, LRU_L
    rows = lax.broadcasted_iota(jnp.int32, (C, L), 0)
    cw = cw_ref[...]

    def body(ci, carry):
        tail, h_prev, _ = carry
        r0 = pl.multiple_of(ci * C, C)
        x = x_ref[pl.ds(r0, C), :]
        xe = jnp.concatenate([tail, x], axis=0)
        c = cb_ref[...] + x * cw[3:4, :]
        for s in (1, 2, 3):
            c = c + pltpu.roll(xe, s, axis=0)[8:, :] * cw[3 - s:4 - s, :]
        a, u = lru_gates(c, wa_ref, ba_ref[...], wx_ref, bx_ref[...], lam_ref[...])
        d = 1
        while d < C:
            keep = rows >= d
            a_s = jnp.where(keep, pltpu.roll(a, d, axis=0), 1.0)
            u_s = jnp.where(keep, pltpu.roll(u, d, axis=0), 0.0)
            u = u + a * u_s
            a = a * a_s
            d *= 2
        h = a * h_prev + u
        o_ref[pl.ds(r0, C), :] = (h * _gelu_tanh(g_ref[pl.ds(r0, C), :])).astype(o_ref.dtype)
        return x[C - 8:, :], h[C - 1:C, :], h[last_row:last_row + 1, :]

    _, _, h_last = lax.fori_loop(0, n_chunks, body, (tail_ref[...], h0_ref[...], h0_ref[...]))
    h_ref[...] = h_last


def lru_mix(lru, tail0, h0, layer, conv_w, conv_b, w_a, b_a, w_x, b_x, lam, *, batch, seq, n_valid, out_rows):
    nj = B_WIDTH // LRU_L
    nb = LRU_L // B_BLOCK
    nc = seq // LRU_ROWS
    ng = batch + (out_rows > batch * seq)
    cb = lambda b: jnp.minimum(b, batch - 1)
    assert 0 <= n_valid - 1 - (nc - 1) * LRU_ROWS < LRU_ROWS
    vec = pl.BlockSpec((None, 1, LRU_L), lambda b, j: (layer, 0, j))
    wblk = pl.BlockSpec((None, nb, B_BLOCK, B_BLOCK), lambda b, j: (layer, j, 0, 0))
    return pl.pallas_call(
        functools.partial(_lru_kernel, n_chunks=nc, last_row=n_valid - 1 - (nc - 1) * LRU_ROWS, batch=batch),
        grid=(ng, nj),
        in_specs=[pl.BlockSpec((seq, LRU_L), lambda b, j: (cb(b), j)),
                  pl.BlockSpec((seq, LRU_L), lambda b, j: (cb(b), nj + j)),
                  pl.BlockSpec((None, 8, LRU_L), lambda b, j: (cb(b), 0, j)),
                  pl.BlockSpec((None, 1, LRU_L), lambda b, j: (cb(b), 0, j)),
                  pl.BlockSpec((None, 4, LRU_L), lambda b, j: (layer, 0, j)),
                  vec, wblk, vec, wblk, vec, vec],
        out_specs=[pl.BlockSpec((seq, LRU_L), lambda b, j: (b, j)),
                   pl.BlockSpec((None, 1, LRU_L), lambda b, j: (b, 0, j))],
        out_shape=[jax.ShapeDtypeStruct((out_rows, B_WIDTH), BF16),
                   jax.ShapeDtypeStruct((ng, 1, B_WIDTH), F32)],
        compiler_params=_cparams(("parallel", "parallel")),
        name="lru_mix",
    )(lru, lru, tail0, h0, conv_w, conv_b, w_a, b_a, w_x, b_x, lam)


C_BAND = 128
NEG_INF = float("-inf")


def _merge_groups(o_prev, lse_prev, o_new, lse_new):
    m = jnp.maximum(lse_prev, lse_new)
    wp = jnp.exp(lse_prev - m)
    wn = jnp.exp(lse_new - m)
    tot = wp + wn
    return (wp * o_prev + wn * o_new) / tot, m + jnp.log(tot)


def _attn_block(q, kcat, vcat, mask):
    sc = jnp.where(mask, _dot_nt(q, kcat) * C_SCALE, NEG_INF)
    m = jnp.max(sc, axis=-1, keepdims=True)
    p = jnp.exp(sc - m)
    den = jnp.sum(p, axis=-1, keepdims=True)
    o = _dot(p, vcat) / den
    return o, jnp.broadcast_to(m + jnp.log(den), o.shape)


def _attn_prompt_kernel(q0_ref, q1_ref, q2_ref, k0_ref, k1_ref, k2_ref, v0_ref, v1_ref, v2_ref, o_ref, oacc, lacc, *, seq,
                        batch):
    @pl.when(pl.program_id(0) == batch)
    def _():
        o_ref[...] = jnp.zeros(o_ref.shape, o_ref.dtype)

    @pl.when(pl.program_id(0) < batch)
    def _():
        _attn_prompt_body(q0_ref, q1_ref, q2_ref, k0_ref, k1_ref, k2_ref, v0_ref, v1_ref, v2_ref, o_ref, oacc, lacc, seq=seq)


def _attn_prompt_body(q0_ref, q1_ref, q2_ref, k0_ref, k1_ref, k2_ref, v0_ref, v1_ref, v2_ref, o_ref, oacc, lacc, *, seq):
    B = C_BAND
    iq = lax.broadcasted_iota(jnp.int32, (B, 2 * B), 0)
    jk = lax.broadcasted_iota(jnp.int32, (B, 2 * B), 1)
    band_ok = (jk >= iq) & (jk <= iq + B)
    iq1 = lax.broadcasted_iota(jnp.int32, (B, B), 0)
    jk1 = lax.broadcasted_iota(jnp.int32, (B, B), 1)
    causal = jk1 <= iq1
    refs = ((q0_ref, k0_ref, v0_ref), (q1_ref, k1_ref, v1_ref), (q2_ref, k2_ref, v2_ref))
    for gi, (_, dil) in enumerate(C_GROUPS):
        q_ref, k_ref, v_ref = refs[gi]
        n_blocks = seq // dil // B
        for r in range(dil):
            for s in range(n_blocks):
                rows = pl.ds(r + dil * B * s, B, stride=dil) if dil > 1 else pl.ds(B * s, B)
                q = q_ref[rows, :]
                if s == 0:
                    o, lse = _attn_block(q, k_ref[rows, :], v_ref[rows, :], causal)
                else:
                    keys = (pl.ds(r + dil * B * (s - 1), 2 * B, stride=dil) if dil > 1 else pl.ds(B * (s - 1), 2 * B))
                    o, lse = _attn_block(q, k_ref[keys, :], v_ref[keys, :], band_ok)
                if gi > 0:
                    o, lse = _merge_groups(oacc[rows, :], lacc[rows, :], o, lse)
                oacc[rows, :] = o
                if gi < C_N_GROUPS - 1:
                    lacc[rows, :] = lse
    o_ref[...] = oacc[...].astype(o_ref.dtype)


def attn_prompt(qn, kn, v, *, batch, seq):
    rows = qn.shape[0]
    nb = batch + (rows > batch * seq)
    spec = lambda gi: pl.BlockSpec((seq, C_HEAD_DIM), lambda b, h: (jnp.minimum(b, batch - 1), gi * C_HEADS + h))
    g = range(C_N_GROUPS)
    return pl.pallas_call(
        functools.partial(_attn_prompt_kernel, seq=seq, batch=batch),
        grid=(nb, C_HEADS),
        in_specs=[spec(gi) for gi in g] * 3,
        out_specs=pl.BlockSpec((seq, C_HEAD_DIM), lambda b, h: (b, h)),
        out_shape=jax.ShapeDtypeStruct((rows, C_GROUP_WIDTH), BF16),
        scratch_shapes=[pltpu.VMEM((seq, C_HEAD_DIM), F32), pltpu.VMEM((seq, C_HEAD_DIM), F32)],
        compiler_params=_cparams(("parallel", "parallel")),
        name="attn_prompt",
    )(qn, qn, qn, kn, kn, kn, v, v, v)


def _attn_sample_kernel(*refs, lb, dil, n_new, first):
    if first:
        q_ref, kn_ref, vn_ref, kc_ref, vc_ref, o_ref, l_ref = refs
    else:
        q_ref, kn_ref, vn_ref, kc_ref, vc_ref, oin_ref, lin_ref, o_ref, l_ref = refs
    R = q_ref.shape[0]
    qi = lax.broadcasted_iota(jnp.int32, (R, lb), 0)
    ci = lax.broadcasted_iota(jnp.int32, (R, lb), 1)
    off = lb + qi - ci
    cache_ok = (off % dil == 0) & (off <= C_BAND * dil)
    qn_i = lax.broadcasted_iota(jnp.int32, (R, R), 0)
    nn_i = lax.broadcasted_iota(jnp.int32, (R, R), 1)
    new_ok = (nn_i <= qn_i) & ((qn_i - nn_i) % dil == 0) & (nn_i < n_new)
    for h in range(C_HEADS):
        ls = slice(h * C_HEAD_DIM, (h + 1) * C_HEAD_DIM)
        q = q_ref[:, ls]
        sc = jnp.where(cache_ok, _dot_nt(q, kc_ref[:, h, :]) * C_SCALE, NEG_INF)
        sn = jnp.where(new_ok, _dot_nt(q, kn_ref[:, ls]) * C_SCALE, NEG_INF)
        m = jnp.maximum(jnp.max(sc, axis=-1, keepdims=True), jnp.max(sn, axis=-1, keepdims=True))
        pc = jnp.exp(sc - m)
        pn = jnp.exp(sn - m)
        den = jnp.sum(pc, axis=-1, keepdims=True) + jnp.sum(pn, axis=-1, keepdims=True)
        o = (_dot(pc, vc_ref[:, h, :]) + _dot(pn, vn_ref[:, ls])) / den
        lse = jnp.broadcast_to(m + jnp.log(den), o.shape)
        if not first:
            o, lse = _merge_groups(oin_ref[:, ls], lin_ref[:, ls], o, lse)
        o_ref[:, ls] = o
        l_ref[:, ls] = lse


def attn_sample_group(qn, kn, v, k_cache, v_cache, o_run, lse_run, *, layer, gi, dil, n_new, first):
    R = 8
    batch = qn.shape[0] // R
    lb = k_cache.shape[2]
    W = C_GROUP_WIDTH
    qmap = lambda b: (b, gi)
    omap = lambda b: (b, 0)
    cmap = lambda b: (layer, b, 0, 0, 0)
    in_specs = ([pl.BlockSpec((R, W), qmap)] * 3
                + [pl.BlockSpec((None, None, lb, C_HEADS, C_HEAD_DIM), cmap)] * 2)
    args = [qn, kn, v, k_cache, v_cache]
    if not first:
        in_specs += [pl.BlockSpec((R, W), omap)] * 2
        args += [o_run, lse_run]
    return pl.pallas_call(
        functools.partial(_attn_sample_kernel, lb=lb, dil=dil, n_new=n_new, first=first),
        grid=(batch,),
        in_specs=in_specs,
        out_specs=[pl.BlockSpec((R, W), omap)] * 2,
        out_shape=[jax.ShapeDtypeStruct((batch * R, W), F32)] * 2,
        compiler_params=_cparams(("parallel",)),
        name="attn_sample_g%d" % gi,
    )(*args)


def _cache_shift_kernel(a_ref, b_ref, f_ref, o_ref, *, n_new, n_blocks):
    tb = o_ref.shape[0]
    o_ref[0:tb - n_new] = a_ref[n_new:tb]

    @pl.when(pl.program_id(1) < n_blocks - 1)
    def _():
        o_ref[tb - n_new:tb] = b_ref[...]

    @pl.when(pl.program_id(1) == n_blocks - 1)
    def _():
        o_ref[tb - n_new:tb] = f_ref[...]


def cache_shift_append(cache, fresh):
    n, lb, hh, e = cache.shape
    n_new = fresh.shape[1]
    tb = min(lb, 512)
    nblk = lb // tb
    nxt = lambda i, j: (i, jnp.minimum((j + 1) * (tb // n_new), lb // n_new - 1), 0, 0)
    return pl.pallas_call(
        functools.partial(_cache_shift_kernel, n_new=n_new, n_blocks=nblk),
        grid=(n, nblk),
        in_specs=[pl.BlockSpec((None, tb, hh, e), lambda i, j: (i, j, 0, 0)),
                  pl.BlockSpec((None, n_new, hh, e), nxt),
                  pl.BlockSpec((None, n_new, hh, e), lambda i, j: (i, 0, 0, 0))],
        out_specs=pl.BlockSpec((None, tb, hh, e), lambda i, j: (i, j, 0, 0)),
        out_shape=jax.ShapeDtypeStruct(cache.shape, cache.dtype),
        compiler_params=_cparams(("parallel", "parallel")),
        name="cache_shift_append",
    )(cache, cache, fresh)


def _pad_last(x, n):
    return jnp.pad(x, [(0, 0)] * (x.ndim - 1) + [(0, n - x.shape[-1])])


def _rows_at(w, row0, rows):
    return jnp.pad(w, ((0, 0), (row0, rows - row0 - w.shape[1]), (0, 0)))


def _sample_rows(t, rows_per_batch):
    s = t[N_PROMPT:N_PROMPT + N_SAMPLE].reshape(DEC_BATCH, DEC_SEQ, t.shape[1])
    return jnp.pad(s, ((0, 0), (0, rows_per_batch - DEC_SEQ), (0, 0))).reshape(DEC_BATCH * rows_per_batch, t.shape[1])


def _with_tail(full, sample_out, rows_per_batch):
    w = full.shape[1]
    s = sample_out.reshape(DEC_BATCH, rows_per_batch, w)[:, :DEC_SEQ].reshape(N_SAMPLE, w).astype(full.dtype)
    tail = jnp.pad(s, ((0, M_PAD - N_PROMPT - N_SAMPLE), (0, 0)))
    return lax.dynamic_update_slice(full, tail, (N_PROMPT, 0))


def _state_to_blockdiag(s):
    b = s.shape[0]
    s5 = s.reshape(b, A_HEADS // 4, 4, A_HEAD_SIZE, A_HEAD_SIZE)
    return jnp.einsum('bgivk,ij->bgivjk', s5, jnp.eye(4, dtype=s.dtype)).reshape(b, A_HEADS // 4, RW_L, RW_L)


def _blockdiag_to_state(s):
    b = s.shape[0]
    s6 = s.reshape(b, A_HEADS // 4, 4, A_HEAD_SIZE, 4, A_HEAD_SIZE)
    return jnp.einsum('bgivik->bgivk', s6).reshape(b, A_HEADS, A_HEAD_SIZE, A_HEAD_SIZE)


def kernel(x_prompt, x_sample, state_rwkv_shift, state_rwkv_wkv, state_lru_conv, state_lru_h, cache_dil1_k, cache_dil1_v, cache_dil2_k, cache_dil2_v, cache_dil3_k, cache_dil3_v, norm_ffn1, ffn1_w_gate, ffn1_w_up, ffn1_w_down, norm_mix, w_in, rwkv_mu, rwkv_w0, rwkv_w_up, rwkv_a0, rwkv_a_up, rwkv_g_up, rwkv_k_k, rwkv_k_a, rwkv_r_k, rwkv_ln_w, rwkv_ln_b, lru_conv_w, lru_conv_b, lru_w_a, lru_b_a, lru_w_x, lru_b_x, lru_lambda, attn_q_norm, attn_k_norm, w_br_a, w_br_b, w_br_c, w_out, norm_ffn2, ffn2_w_gate, ffn2_w_up, ffn2_w_down):
    L = DEPTH
    caches = ((cache_dil1_k, cache_dil1_v), (cache_dil2_k, cache_dil2_v), (cache_dil3_k, cache_dil3_v))

    wd1, wd2 = ffn1_w_down.astype(BF16), ffn2_w_down.astype(BF16)
    w_in_t = jnp.swapaxes(w_in, 1, 2)
    w_rest = cast_rows(w_in_t, A_SHIFT_WIDTH)
    o = tuple(c - A_SHIFT_WIDTH for c in IN_OFFSETS)
    wba, wbb, wbc, wo = (w.astype(BF16) for w in (w_br_a, w_br_b, w_br_c, w_out))

    vec = lambda t: t.reshape(L, 1, -1)
    g1, gm, g2 = vec(norm_ffn1), vec(norm_mix), vec(norm_ffn2)
    zw0 = A_SPLIT_OFFSETS[2]
    za0, zg0 = A_SPLIT_OFFSETS[3] - zw0, A_SPLIT_OFFSETS[4] - zw0
    g_split = RW_L - zg0
    rw_params = (vec(_pad_last(rwkv_mu, PA_W)), vec(rwkv_w0), _rows_at(rwkv_w_up, 0, RW_L), vec(rwkv_a0),
                 _rows_at(rwkv_a_up, za0, RW_L), _rows_at(rwkv_g_up[:, :g_split], zg0, RW_L),
                 _rows_at(rwkv_g_up[:, g_split:], 0, RW_L), vec(rwkv_k_k), vec(rwkv_k_a), vec(rwkv_r_k),
                 vec(rwkv_ln_w), vec(rwkv_ln_b))
    lru_params = (lru_conv_w, vec(lru_conv_b), lru_w_a, vec(lru_b_a), lru_w_x, vec(lru_b_x), vec(lru_lambda))
    q_gain = attn_q_norm.reshape(L, C_N_GROUPS, 1, C_HEAD_DIM)
    k_gain = attn_k_norm.reshape(L, C_N_GROUPS, 1, C_HEAD_DIM)

    shift_p0 = jnp.zeros((BATCH, 1, PA_W), F32)
    wkv_p0 = jnp.zeros((BATCH, A_WIDTH // RW_L, RW_L, RW_L), F32)
    tail_p0 = jnp.zeros((BATCH, 8, B_WIDTH), F32)
    h_p0 = jnp.zeros((BATCH, 1, B_WIDTH), F32)

    x = jnp.concatenate([x_prompt.reshape(N_PROMPT, D_MODEL), x_sample.reshape(N_SAMPLE, D_MODEL),
                         jnp.zeros((M_PAD - N_PROMPT - N_SAMPLE, D_MODEL), F32)], axis=0)

    outs_p, outs_s = [], []
    for l in range(L):
        u = rmsnorm_rows(x, g1, l)
        hff = ffn_gateup(u, ffn1_w_gate, ffn1_w_up, l)
        h = matmul_resid(hff, wd1, x, l, 512, D_FF // 2, 0.5, "ffn1_down")
        u = rmsnorm_rows(h, gm, l)
        pa = matmul_fullk(u, w_in_t, l, 512, F32, "proj_rwkv", n=PA_W)
        lru = matmul_fullk(u, w_rest, l, 1024, F32, "proj_lru", col0=o[0], n=2 * B_WIDTH)
        qn = matmul_qknorm(u, w_rest, q_gain, l, "proj_q", col0=o[2])
        kn = matmul_qknorm(u, w_rest, k_gain, l, "proj_k", col0=o[3])
        vv = matmul_fullk(u, w_rest, l, 1024, F32, "proj_v", col0=o[4], n=C_QKV_WIDTH)
        gates = matmul_fullk(u, w_rest, l, 1024, F32, "proj_gates", col0=o[5], n=3 * D_MODEL)

        oa, wkv_p = rwkv_mix(pa, shift_p0, wkv_p0, l, *rw_params, batch=BATCH, seq=SEQ, n_valid=SEQ, out_rows=M_PAD)
        oa_s, wkv_s = rwkv_mix(_sample_rows(pa, RW_C), _pad_last(state_rwkv_shift[l], PA_W)[:, None, :],
                               _state_to_blockdiag(state_rwkv_wkv[l]), l, *rw_params, batch=DEC_BATCH, seq=RW_C,
                               n_valid=DEC_SEQ, out_rows=DEC_BATCH * RW_C)
        oa = _with_tail(oa, oa_s, RW_C)
        shift_p = pa[SEQ - 1:N_PROMPT:SEQ]
        shift_s = pa[N_PROMPT + DEC_SEQ - 1:N_PROMPT + N_SAMPLE:DEC_SEQ]

        ob, lh_p = lru_mix(lru, tail_p0, h_p0, l, *lru_params, batch=BATCH, seq=SEQ, n_valid=SEQ, out_rows=M_PAD)
        tail_s = jnp.pad(state_lru_conv[l], ((0, 0), (8 - (CONV_WIDTH - 1), 0), (0, 0)))
        ob_s, lh_s = lru_mix(_sample_rows(lru, LRU_ROWS), tail_s, state_lru_h[l][:, None, :], l, *lru_params,
                             batch=DEC_BATCH, seq=LRU_ROWS, n_valid=DEC_SEQ, out_rows=DEC_BATCH * LRU_ROWS)
        ob = _with_tail(ob, ob_s, LRU_ROWS)
        lx_s = lru[N_PROMPT:N_PROMPT + N_SAMPLE, :B_WIDTH].reshape(DEC_BATCH, DEC_SEQ, B_WIDTH)
        conv_p = jnp.stack([lru[(b + 1) * SEQ - (CONV_WIDTH - 1):(b + 1) * SEQ, :B_WIDTH] for b in range(BATCH)])
        conv_s = jnp.concatenate([state_lru_conv[l], lx_s], axis=1)[:, -(CONV_WIDTH - 1):]

        oc = attn_prompt(qn, kn, vv, batch=BATCH, seq=SEQ)
        qs, ks, vs = _sample_rows(qn, 8), _sample_rows(kn, 8), _sample_rows(vv, 8)
        o_run = lse_run = None
        for gi, (window, dil) in enumerate(C_GROUPS):
            kc, vc = caches[gi]
            o_run, lse_run = attn_sample_group(qs, ks, vs, kc, vc, o_run, lse_run, layer=l, gi=gi, dil=dil,
                                               n_new=DEC_SEQ, first=gi == 0)
        oc = _with_tail(oc, o_run, 8)

        new_p = [shift_p[:, :A_SHIFT_WIDTH], _blockdiag_to_state(wkv_p[:BATCH]), conv_p, lh_p[:BATCH, 0]]
        new_s = [shift_s[:, :A_SHIFT_WIDTH], _blockdiag_to_state(wkv_s), conv_s, lh_s[:, 0]]
        for gi, (window, _) in enumerate(C_GROUPS):
            keep = min(window, SEQ)
            cols = slice(gi * C_GROUP_WIDTH, (gi + 1) * C_GROUP_WIDTH)
            for t, cache in ((kn, caches[gi][0]), (vv, caches[gi][1])):
                kept = jnp.stack([t[(b + 1) * SEQ - keep:(b + 1) * SEQ, cols] for b in range(BATCH)])
                new_p.append(kept.reshape(BATCH, keep, C_HEADS, C_HEAD_DIM))
                new_s.append(t[N_PROMPT:N_PROMPT + N_SAMPLE, cols].reshape(DEC_BATCH, DEC_SEQ, C_HEADS, C_HEAD_DIM))
        outs_p.append(new_p)
        outs_s.append(new_s)

        merged = branch_merge(oa, ob, oc, wba, wbb, wbc, gates, l)
        h = matmul_resid(merged, wo, h, l, 512, D_MODEL, 1.0, "w_out")
        u = rmsnorm_rows(h, g2, l)
        hff = ffn_gateup(u, ffn2_w_gate, ffn2_w_up, l)
        x = matmul_resid(hff, wd2, h, l, 512, D_FF // 2, 0.5, "ffn2_down")

    res = [x[:N_PROMPT].reshape(BATCH, SEQ, D_MODEL),
           x[N_PROMPT:N_PROMPT + N_SAMPLE].reshape(DEC_BATCH, DEC_SEQ, D_MODEL)]
    flat_caches = [c for pair in caches for c in pair]
    for i in range(10):
        res.append(jnp.stack([o_[i] for o_ in outs_p]))
        s_i = jnp.stack([o_[i] for o_ in outs_s])
        if i >= 4:
            cache = flat_caches[i - 4]
            lb = cache.shape[2]
            s_i = cache_shift_append(cache.reshape(L * DEC_BATCH, lb, C_HEADS, C_HEAD_DIM),
                                     s_i.reshape(L * DEC_BATCH, DEC_SEQ, C_HEADS, C_HEAD_DIM)).reshape(cache.shape)
        res.append(s_i)
    return tuple(res)
```
